```python
import math
import jax, jax.numpy as jnp
from jax import lax
import numpy as np

D_MODEL = 1024
BATCH = 16
SEQ = 2048
DEPTH = 1

SSD_HEADS = 16
SSD_HEAD_DIM = 64
SSD_INNER = SSD_HEADS * SSD_HEAD_DIM
SSD_GROUPS = 2
SSD_STATE = 128
SSD_CONV = 4
SSD_CHUNK = 128
SSD_CONV_CH = SSD_INNER + 2 * SSD_GROUPS * SSD_STATE
MLA_HEADS = 16
MLA_Q_RANK = 384
MLA_KV_RANK = 256
MLA_NOPE = 64
MLA_ROPE = 32
MLA_V = 64
MLA_QK = MLA_NOPE + MLA_ROPE
ROPE_THETA = 10000.0
Q_BLOCK = 128
MEM_LEN = 256
XA_HEADS = 4
XA_HEAD_DIM = D_MODEL // XA_HEADS
D_FF = 2816
FFN_RES_WEIGHT = 0.5
N_BRANCHES = 2
EPS = 1e-6
IN_SIZES = (SSD_INNER, SSD_CONV_CH, SSD_HEADS, MLA_Q_RANK, MLA_KV_RANK, MLA_ROPE, N_BRANCHES * D_MODEL)
D_IN = SSD_INNER + SSD_CONV_CH + SSD_HEADS + MLA_Q_RANK + MLA_KV_RANK + MLA_ROPE + N_BRANCHES * D_MODEL

kernel_name = "hybrid_ssd_mla_gated_macaron"


def _split_points(sizes):
    pts, acc = [], 0
    for sz in sizes[:-1]:
        acc += sz
        pts.append(acc)
    return pts


def rms_norm(x, g):
    xf = x.astype(jnp.float32)
    y = xf * lax.rsqrt(jnp.mean(xf * xf, axis=-1, keepdims=True) + EPS)
    return (y * g.astype(jnp.float32)).astype(x.dtype)


def swiglu(h, w_gate, w_up, w_down):
    return (jax.nn.silu(h @ w_gate) * (h @ w_up)) @ w_down


def rope_cos_sin(positions, dim):
    inv = ROPE_THETA ** (-jnp.arange(0, dim, 2, dtype=jnp.float32) / dim)
    ang = positions.astype(jnp.float32)[..., None] * inv
    return jnp.cos(ang), jnp.sin(ang)


def apply_rope(x, cos, sin):
    x1, x2 = jnp.split(x.astype(jnp.float32), 2, axis=-1)
    return jnp.concatenate([x1 * cos - x2 * sin, x1 * sin + x2 * cos], axis=-1).astype(x.dtype)


def causal_depthwise_conv(u, w, b):
    out = lax.conv_general_dilated(u, w[:, None, :], window_strides=(1,),
                                   padding=((SSD_CONV - 1, 0),),
                                   dimension_numbers=('NWC', 'WIO', 'NWC'),
                                   feature_group_count=u.shape[-1])
    return out + b


def segsum(a):
    L = a.shape[-1]
    cs = jnp.cumsum(a, axis=-1)
    diff = cs[..., :, None] - cs[..., None, :]
    mask = jnp.tril(jnp.ones((L, L), dtype=bool))
    return jnp.where(mask, diff, -jnp.inf)


def ssd_chunked(xh, dt, a, bm, cm):
    bsz, s, h, p = xh.shape
    g, n = bm.shape[-2:]
    r = h // g
    L = SSD_CHUNK
    c = s // L
    f32 = jnp.float32
    xdt = (xh.astype(f32) * dt[..., None]).reshape(bsz, c, L, g, r, p)
    adt = (dt * a).reshape(bsz, c, L, g, r).transpose(0, 3, 4, 1, 2)
    bm = bm.astype(f32).reshape(bsz, c, L, g, n)
    cm = cm.astype(f32).reshape(bsz, c, L, g, n)
    a_cs = jnp.cumsum(adt, axis=-1)
    decay = jnp.exp(segsum(adt))
    cb = jnp.einsum('bclgn,bcsgn->bcgls', cm, bm)
    y_diag = jnp.einsum('bcgls,bgrcls,bcsgrp->bclgrp', cb, decay, xdt)
    decay_states = jnp.exp(a_cs[..., -1:] - a_cs)
    states = jnp.einsum('bclgn,bgrcl,bclgrp->bcgrpn', bm, decay_states, xdt)
    chunk_decay = jnp.exp(a_cs[..., -1])

    def step(carry, inp):
        st, dec = inp
        return carry * dec[..., None, None] + st, carry

    init = jnp.zeros((bsz, g, r, p, n), f32)
    _, prev = lax.scan(step, init, (states.transpose(1, 0, 2, 3, 4, 5), chunk_decay.transpose(3, 0, 1, 2)))
    prev = prev.transpose(1, 0, 2, 3, 4, 5)
    y_off = jnp.einsum('bclgn,bcgrpn,bgrcl->bclgrp', cm, prev, jnp.exp(a_cs))
    return (y_diag + y_off).reshape(bsz, s, h, p)


def causal_block_attention(q_nope, q_rope, k_nope, k_rope, v):
    bsz, s, h, _ = q_nope.shape
    nb = s // Q_BLOCK
    scale = MLA_QK ** -0.5
    kpos = jnp.arange(s)

    def blk(args):
        qn, qr, i = args
        sc = (jnp.einsum('bqhd,bkhd->bhqk', qn, k_nope)
              + jnp.einsum('bqhd,bkd->bhqk', qr, k_rope)).astype(jnp.float32) * scale
        qpos = i * Q_BLOCK + jnp.arange(Q_BLOCK)
        sc = jnp.where(kpos[None, :] <= qpos[:, None], sc, -jnp.inf)
        pr = jax.nn.softmax(sc, axis=-1).astype(v.dtype)
        return jnp.einsum('bhqk,bkhd->bqhd', pr, v)

    def to_blocks(t):
        return t.reshape(bsz, nb, Q_BLOCK, *t.shape[2:]).swapaxes(0, 1)

    out = lax.map(blk, (to_blocks(q_nope), to_blocks(q_rope), jnp.arange(nb)))
    return out.swapaxes(0, 1).reshape(bsz, s, h, -1)


def hybrid_mixer(h, positions, w_in, conv_w, conv_b, dt_bias, a_log, d_skip, ssd_norm_g, w_ssd_proj,
                 q_norm_g, w_uq, kv_norm_g, w_uk, w_uv, w_mla_proj, gate_bias, w_out):
    bsz, s, _ = h.shape
    f32 = jnp.float32
    proj = h @ w_in
    z, xbc, dt_raw, q_c, kv_c, k_r, gate_logits = jnp.split(proj, _split_points(IN_SIZES), axis=-1)

    xbc = jax.nn.silu(causal_depthwise_conv(xbc, conv_w, conv_b))
    xs, bm, cm = jnp.split(xbc, [SSD_INNER, SSD_INNER + SSD_GROUPS * SSD_STATE], axis=-1)
    xs = xs.reshape(bsz, s, SSD_HEADS, SSD_HEAD_DIM)
    bm = bm.reshape(bsz, s, SSD_GROUPS, SSD_STATE)
    cm = cm.reshape(bsz, s, SSD_GROUPS, SSD_STATE)
    dt = jax.nn.softplus((dt_raw + dt_bias).astype(f32))
    a = -jnp.exp(a_log.astype(f32))
    y = ssd_chunked(xs, dt, a, bm, cm) + d_skip.astype(f32)[:, None] * xs.astype(f32)
    y = y.reshape(bsz, s, SSD_INNER).astype(h.dtype) * jax.nn.silu(z)
    y = rms_norm(y.reshape(bsz, s, SSD_GROUPS, -1), ssd_norm_g.reshape(SSD_GROUPS, -1)).reshape(bsz, s, SSD_INNER)
    y_ssd = y @ w_ssd_proj

    cos, sin = rope_cos_sin(positions, MLA_ROPE)
    q = (rms_norm(q_c, q_norm_g) @ w_uq).reshape(bsz, s, MLA_HEADS, MLA_QK)
    q_nope = q[..., :MLA_NOPE]
    q_rope = apply_rope(q[..., MLA_NOPE:], cos[:, :, None], sin[:, :, None])
    kv_c = rms_norm(kv_c, kv_norm_g)
    k_nope = (kv_c @ w_uk).reshape(bsz, s, MLA_HEADS, MLA_NOPE)
    v = (kv_c @ w_uv).reshape(bsz, s, MLA_HEADS, MLA_V)
    k_rope = apply_rope(k_r, cos, sin)
    o = causal_block_attention(q_nope, q_rope, k_nope, k_rope, v).reshape(bsz, s, MLA_HEADS * MLA_V)
    y_mla = o @ w_mla_proj

    gates = jax.nn.sigmoid((gate_logits + gate_bias).astype(f32)).astype(h.dtype)
    g_ssd, g_mla = jnp.split(gates, N_BRANCHES, axis=-1)
    return (g_ssd * y_ssd + g_mla * y_mla) @ w_out


def memory_cross_attention(h, mem_n, w_q, w_k, w_v, w_o):
    bsz, s, _ = h.shape
    q = (h @ w_q).reshape(bsz, s, XA_HEADS, XA_HEAD_DIM)
    k = (mem_n @ w_k).reshape(bsz, -1, XA_HEADS, XA_HEAD_DIM)
    v = (mem_n @ w_v).reshape(bsz, -1, XA_HEADS, XA_HEAD_DIM)
    sc = jnp.einsum('bqhd,bkhd->bhqk', q, k).astype(jnp.float32) * (XA_HEAD_DIM ** -0.5)
    pr = jax.nn.softmax(sc, axis=-1).astype(v.dtype)
    o = jnp.einsum('bhqk,bkhd->bqhd', pr, v).reshape(bsz, s, D_MODEL)
    return o @ w_o


def setup_inputs(seed: int = 0) -> dict:
    key = jax.random.key(seed)
    keys = iter(jax.random.split(key, 48))
    f32 = jnp.float32

    def dense(fan_in, *shape):
        return jax.random.normal(next(keys), (DEPTH,) + shape, f32) * fan_in ** -0.5

    def gain(*shape):
        return 1.0 + 0.02 * jax.random.normal(next(keys), (DEPTH,) + shape, f32)

    def small(*shape):
        return 0.01 * jax.random.normal(next(keys), (DEPTH,) + shape, f32)

    x = jax.random.normal(next(keys), (BATCH, SEQ, D_MODEL), f32)
    mem = jax.random.normal(next(keys), (BATCH, MEM_LEN, D_MODEL), f32)
    offset = jax.random.randint(next(keys), (BATCH, 1), 0, 1024, dtype=jnp.int32)
    positions = (offset + jnp.arange(SEQ, dtype=jnp.int32)[None, :]).astype(jnp.int32)

    u = jax.random.uniform(next(keys), (DEPTH, SSD_HEADS), f32)
    dt0 = jnp.exp(u * (math.log(0.1) - math.log(0.001)) + math.log(0.001))
    dt_bias = dt0 + jnp.log(-jnp.expm1(-dt0))
    a_log = jnp.log(jax.random.uniform(next(keys), (DEPTH, SSD_HEADS), f32, minval=1.0, maxval=16.0))

    return {
        "x": x, "mem": mem, "positions": positions,
        "ffn1_pre_g": gain(D_MODEL), "ffn1_w_gate": dense(D_MODEL, D_MODEL, D_FF),
        "ffn1_w_up": dense(D_MODEL, D_MODEL, D_FF), "ffn1_w_down": dense(D_FF, D_FF, D_MODEL),
        "ffn1_post_g": gain(D_MODEL),
        "mix_pre_g": gain(D_MODEL), "w_in": dense(D_MODEL, D_MODEL, D_IN),
        "conv_w": dense(SSD_CONV, SSD_CONV, SSD_CONV_CH), "conv_b": small(SSD_CONV_CH),
        "dt_bias": dt_bias, "a_log": a_log,
        "d_skip": 1.0 + 0.1 * jax.random.normal(next(keys), (DEPTH, SSD_HEADS), f32),
        "ssd_norm_g": gain(SSD_INNER), "w_ssd_proj": dense(SSD_INNER, SSD_INNER, D_MODEL),
        "q_norm_g": gain(MLA_Q_RANK), "w_uq": dense(MLA_Q_RANK, MLA_Q_RANK, MLA_HEADS * MLA_QK),
        "kv_norm_g": gain(MLA_KV_RANK), "w_uk": dense(MLA_KV_RANK, MLA_KV_RANK, MLA_HEADS * MLA_NOPE),
        "w_uv": dense(MLA_KV_RANK, MLA_KV_RANK, MLA_HEADS * MLA_V),
        "w_mla_proj": dense(MLA_HEADS * MLA_V, MLA_HEADS * MLA_V, D_MODEL),
        "gate_bias": small(N_BRANCHES * D_MODEL), "w_out": dense(D_MODEL, D_MODEL, D_MODEL),
        "mix_post_g": gain(D_MODEL),
        "xa_pre_g": gain(D_MODEL), "mem_norm_g": gain(D_MODEL),
        "w_xq": dense(D_MODEL, D_MODEL, D_MODEL), "w_xk": dense(D_MODEL, D_MODEL, D_MODEL),
        "w_xv": dense(D_MODEL, D_MODEL, D_MODEL), "w_xo": dense(D_MODEL, D_MODEL, D_MODEL),
        "xa_post_g": gain(D_MODEL),
        "ffn2_pre_g": gain(D_MODEL), "ffn2_w_gate": dense(D_MODEL, D_MODEL, D_FF),
        "ffn2_w_up": dense(D_MODEL, D_MODEL, D_FF), "ffn2_w_down": dense(D_FF, D_FF, D_MODEL),
        "ffn2_post_g": gain(D_MODEL),
    }


def reference(x, mem, positions, ffn1_pre_g, ffn1_w_gate, ffn1_w_up, ffn1_w_down, ffn1_post_g,
              mix_pre_g, w_in, conv_w, conv_b, dt_bias, a_log, d_skip, ssd_norm_g, w_ssd_proj,
              q_norm_g, w_uq, kv_norm_g, w_uk, w_uv, w_mla_proj, gate_bias, w_out, mix_post_g,
              xa_pre_g, mem_norm_g, w_xq, w_xk, w_xv, w_xo, xa_post_g,
              ffn2_pre_g, ffn2_w_gate, ffn2_w_up, ffn2_w_down, ffn2_post_g):
    for l in range(DEPTH):
        h = swiglu(rms_norm(x, ffn1_pre_g[l]), ffn1_w_gate[l], ffn1_w_up[l], ffn1_w_down[l])
        x = x + FFN_RES_WEIGHT * rms_norm(h, ffn1_post_g[l])
        h = hybrid_mixer(rms_norm(x, mix_pre_g[l]), positions, w_in[l], conv_w[l], conv_b[l], dt_bias[l],
                         a_log[l], d_skip[l], ssd_norm_g[l], w_ssd_proj[l], q_norm_g[l], w_uq[l],
                         kv_norm_g[l], w_uk[l], w_uv[l], w_mla_proj[l], gate_bias[l], w_out[l])
        x = x + rms_norm(h, mix_post_g[l])
        h = memory_cross_attention(rms_norm(x, xa_pre_g[l]), rms_norm(mem, mem_norm_g[l]),
                                   w_xq[l], w_xk[l], w_xv[l], w_xo[l])
        x = x + rms_norm(h, xa_post_g[l])
        h = swiglu(rms_norm(x, ffn2_pre_g[l]), ffn2_w_gate[l], ffn2_w_up[l], ffn2_w_down[l])
        x = x + FFN_RES_WEIGHT * rms_norm(h, ffn2_post_g[l])
    return x
```

```python
import functools

import jax
import jax.numpy as jnp
from jax import lax
from jax.experimental import pallas as pl
from jax.experimental.pallas import tpu as pltpu

F32 = jnp.float32
BF16 = jnp.bfloat16

D_MODEL = 1024
SSD_HEADS = 16
SSD_HEAD_DIM = 64
SSD_INNER = SSD_HEADS * SSD_HEAD_DIM
SSD_GROUPS = 2
SSD_STATE = 128
SSD_CONV = 4
SSD_CHUNK = 128
SSD_CONV_CH = SSD_INNER + 2 * SSD_GROUPS * SSD_STATE
MLA_HEADS = 16
MLA_Q_RANK = 384
MLA_KV_RANK = 256
MLA_NOPE = 64
MLA_ROPE = 32
MLA_V = 64
MLA_QK = MLA_NOPE + MLA_ROPE
ROPE_THETA = 10000.0
XA_HEADS = 4
XA_HEAD_DIM = D_MODEL // XA_HEADS
D_FF = 2816
FFN_RES_WEIGHT = 0.5
EPS = 1e-6

LANES = 128
HEAD_PAD = LANES
ROPE_HALF = MLA_ROPE // 2
GROUP_W = SSD_INNER // SSD_GROUPS
PAIRS_PER_GROUP = GROUP_W // LANES

TM = 512
FFN_CHUNK = 256
TS = 512
TQ = 256
CONV_PAD = 8
VMEM_LIMIT = 56 * 1024 * 1024

_C_Z = (0, SSD_INNER)
_C_XBC = (_C_Z[1], _C_Z[1] + SSD_CONV_CH)
_C_GATE = (_C_XBC[1], _C_XBC[1] + 2 * D_MODEL)
_C_QC = (_C_GATE[1], _C_GATE[1] + MLA_Q_RANK)
_C_KVC = (_C_QC[1], _C_QC[1] + MLA_KV_RANK)
_C_KR = (_C_KVC[1], _C_KVC[1] + HEAD_PAD)
_C_DT = (_C_KR[1], _C_KR[1] + LANES)
D_IN_ARR = _C_DT[1]


def _rms(x, g):
    return x * lax.rsqrt(jnp.mean(x * x, axis=-1, keepdims=True) + EPS) * g


def _softplus(x):
    return jnp.maximum(x, 0.0) + jnp.log1p(jnp.exp(-jnp.abs(x)))


def _resident(shape):
    zeros = (0,) * len(shape)
    return pl.BlockSpec(shape, lambda *_: zeros, pipeline_mode=pl.Buffered(1))


def _params(*sem):
    return pltpu.CompilerParams(dimension_semantics=sem, vmem_limit_bytes=VMEM_LIMIT)


def _ffn_kernel(x_ref, pre_g_ref, wg_ref, wu_ref, wd_ref, post_g_ref, o_ref, h_ref):
    x = x_ref[...]
    xn = _rms(x, pre_g_ref[...]).astype(BF16)
    for c in range(D_FF // FFN_CHUNK):
        sl = slice(c * FFN_CHUNK, (c + 1) * FFN_CHUNK)
        g = jnp.dot(xn, wg_ref[:, sl], preferred_element_type=F32)
        u = jnp.dot(xn, wu_ref[:, sl], preferred_element_type=F32)
        h_ref[:, sl] = (jax.nn.silu(g) * u).astype(BF16)
    y = jnp.dot(h_ref[...], wd_ref[...], preferred_element_type=F32)
    o_ref[...] = x + FFN_RES_WEIGHT * _rms(y, post_g_ref[...])


def _ffn(x, pre_g, wg, wu, wd, post_g):
    t = x.shape[0]
    tok = pl.BlockSpec((TM, D_MODEL), lambda i: (i, 0))
    return pl.pallas_call(
        _ffn_kernel,
        grid=(t // TM,),
        in_specs=[tok, _resident((1, D_MODEL)), _resident((D_MODEL, D_FF)), _resident((D_MODEL, D_FF)),
                  _resident((D_FF, D_MODEL)), _resident((1, D_MODEL))],
        out_specs=tok,
        out_shape=jax.ShapeDtypeStruct((t, D_MODEL), F32),
        scratch_shapes=[pltpu.VMEM((TM, D_FF), BF16)],
        compiler_params=_params("parallel"),
        name="ffn",
    )(x, pre_g, wg, wu, wd, post_g)


def _inproj_kernel(x_ref, g_ref, w_ref, z_ref, xbc_ref, gate_ref, qc_ref, kvc_ref, kr_ref, dt_ref):
    hn = _rms(x_ref[...], g_ref[...]).astype(BF16)

    def proj(cols):
        return jnp.dot(hn, w_ref[:, cols[0]:cols[1]], preferred_element_type=F32)

    z_ref[...] = proj(_C_Z).astype(BF16)
    xbc_ref[...] = proj(_C_XBC).astype(BF16)
    gate_ref[...] = proj(_C_GATE).astype(BF16)
    qc_ref[...] = proj(_C_QC).astype(BF16)
    kvc_ref[...] = proj(_C_KVC).astype(BF16)
    kr_ref[...] = proj(_C_KR)
    dt_ref[...] = proj(_C_DT)


def _inproj(x, g, w_arr):
    t = x.shape[0]

    def tok(width):
        return pl.BlockSpec((TM, width), lambda i: (i, 0))

    widths = (SSD_INNER, SSD_CONV_CH, 2 * D_MODEL, MLA_Q_RANK, MLA_KV_RANK, HEAD_PAD, LANES)
    dtypes = (BF16, BF16, BF16, BF16, BF16, F32, F32)
    return pl.pallas_call(
        _inproj_kernel,
        grid=(t // TM,),
        in_specs=[tok(D_MODEL), _resident((1, D_MODEL)), _resident((D_MODEL, D_IN_ARR))],
        out_specs=[tok(w) for w in widths],
        out_shape=[jax.ShapeDtypeStruct((t, w), d) for w, d in zip(widths, dtypes)],
        compiler_params=_params("parallel"),
        name="inproj",
    )(x, g, w_arr)


def _ssd_kernel(xbc_ref, z_ref, dt_ref, cw_ref, cb_ref, dtb_ref, alog_ref, dskip_ref, ng_ref,
                y_ref, xpad, u_scr, state):
    L = SSD_CHUNK

    @pl.when(pl.program_id(1) == 0)
    def _start_of_sequence():
        xpad[0:CONV_PAD, :] = jnp.zeros((CONV_PAD, SSD_CONV_CH), F32)
        state[...] = jnp.zeros_like(state)

    xpad[CONV_PAD:CONV_PAD + TS, :] = xbc_ref[...].astype(F32)
    u_all = cb_ref[...]
    for k in range(SSD_CONV):
        first_row = CONV_PAD - (SSD_CONV - 1) + k
        u_all = u_all + cw_ref[k:k + 1, :] * xpad[first_row:first_row + TS, :]
    u_scr[...] = jax.nn.silu(u_all)
    xpad[0:CONV_PAD, :] = xpad[TS:TS + CONV_PAD, :]

    lane = lax.broadcasted_iota(jnp.int32, (L, LANES), 1)
    low_half = lane < SSD_HEAD_DIM
    tril = lax.broadcasted_iota(jnp.int32, (L, L), 1) <= lax.broadcasted_iota(jnp.int32, (L, L), 0)
    tri_f = tril.astype(F32)
    a_neg = -jnp.exp(alog_ref[...])

    def chunk(c, carry):
        r0 = pl.multiple_of(c * L, L)
        u = u_scr[pl.ds(r0, L), :]
        xs = u[:, :SSD_INNER]
        bmat = u[:, SSD_INNER:SSD_INNER + SSD_GROUPS * SSD_STATE]
        cmat = u[:, SSD_INNER + SSD_GROUPS * SSD_STATE:]
        dt = _softplus(dt_ref[pl.ds(r0, L), :] + dtb_ref[...])
        adt = jnp.where(lane < SSD_HEADS, dt * a_neg, 0.0)
        cs = jnp.dot(tri_f, adt, precision=lax.Precision.HIGHEST, preferred_element_type=F32)
        cs_t = cs.T
        z = z_ref[pl.ds(r0, L), :].astype(F32)

        for g in range(SSD_GROUPS):
            bm_g = bmat[:, g * SSD_STATE:(g + 1) * SSD_STATE]
            cm_g = cmat[:, g * SSD_STATE:(g + 1) * SSD_STATE].astype(BF16)
            cbm = lax.dot_general(cm_g, bm_g.astype(BF16), (((1,), (1,)), ((), ())),
                                  preferred_element_type=F32)
            s_g = state[g]
            y_off = jnp.dot(cm_g, s_g.astype(BF16), preferred_element_type=F32)
            xw_parts, cd_parts, y_parts = [], [], []
            for pp in range(PAIRS_PER_GROUP):
                p = g * PAIRS_PER_GROUP + pp
                h0, h1 = 2 * p, 2 * p + 1
                cols = slice(p * LANES, (p + 1) * LANES)

                def pair(v):
                    return jnp.where(low_half, v[:, h0:h0 + 1], v[:, h1:h1 + 1])

                def decay(h):
                    return jnp.exp(jnp.where(tril, cs[:, h:h + 1] - cs_t[h:h + 1, :], -jnp.inf))

                dt_p = pair(dt)
                cs_p = pair(cs)
                xs_p = xs[:, cols]
                xdt_p = xs_p * dt_p
                ecs_p = jnp.exp(cs_p)
                cs_last = cs_p[L - 1:L, :]
                xw_parts.append((xdt_p * jnp.exp(cs_last - cs_p)).astype(BF16))
                cd_parts.append(ecs_p[L - 1:L, :])
                m2 = jnp.concatenate([(cbm * decay(h0)).astype(BF16), (cbm * decay(h1)).astype(BF16)], axis=1)
                x_blk = jnp.concatenate([jnp.where(low_half, xdt_p, 0.0), jnp.where(low_half, 0.0, xdt_p)],
                                        axis=0).astype(BF16)
                y_p = jnp.dot(m2, x_blk, preferred_element_type=F32)
                y_p = y_p + y_off[:, pp * LANES:(pp + 1) * LANES] * ecs_p + dskip_ref[:, cols] * xs_p
                y_parts.append(y_p * jax.nn.silu(z[:, cols]))
            xw_g = jnp.concatenate(xw_parts, axis=1)
            cd_g = jnp.concatenate(cd_parts, axis=1)
            state[g] = s_g * cd_g + jnp.dot(bm_g.T.astype(BF16), xw_g, preferred_element_type=F32)
            y_g = jnp.concatenate(y_parts, axis=1)
            gcols = slice(g * GROUP_W, (g + 1) * GROUP_W)
            y_ref[pl.ds(r0, L), gcols] = _rms(y_g, ng_ref[:, gcols]).astype(BF16)
        return carry

    lax.fori_loop(0, TS // L, chunk, 0)


def _ssd(xbc, z, dt, conv_w, conv_b, dt_bias_pad, a_log_pad, d_skip_exp, norm_g, batch, seq):
    t = xbc.shape[0]
    nblk = seq // TS

    def tok(width):
        return pl.BlockSpec((TS, width), lambda b, j: (b * nblk + j, 0))

    return pl.pallas_call(
        _ssd_kernel,
        grid=(batch, nblk),
        in_specs=[tok(SSD_CONV_CH), tok(SSD_INNER), tok(LANES),
                  _resident((SSD_CONV, SSD_CONV_CH)), _resident((1, SSD_CONV_CH)),
                  _resident((1, LANES)), _resident((1, LANES)),
                  _resident((1, SSD_INNER)), _resident((1, SSD_INNER))],
        out_specs=tok(SSD_INNER),
        out_shape=jax.ShapeDtypeStruct((t, SSD_INNER), BF16),
        scratch_shapes=[pltpu.VMEM((TS + CONV_PAD, SSD_CONV_CH), F32),
                        pltpu.VMEM((TS, SSD_CONV_CH), F32),
                        pltpu.VMEM((SSD_GROUPS, SSD_STATE, GROUP_W), F32)],
        compiler_params=_params("parallel", "arbitrary"),
        name="ssd",
    )(xbc, z, dt, conv_w, conv_b, dt_bias_pad, a_log_pad, d_skip_exp, norm_g)


def _mla_prep_kernel(qc_ref, kvc_ref, kr_ref, pos_ref, inv_ref, qg_ref, kvg_ref, wuq_ref, wuk_ref, wuv_ref,
                     q_out, k_out, v_out):
    ang = pos_ref[...].astype(F32) * inv_ref[...]
    cos = jnp.cos(ang)
    sin = jnp.sin(ang)
    lane = lax.broadcasted_iota(jnp.int32, ang.shape, 1)
    first = (lane >= MLA_NOPE) & (lane < MLA_NOPE + ROPE_HALF)
    second = (lane >= MLA_NOPE + ROPE_HALF) & (lane < MLA_QK)
    c_tab = jnp.where(first | second, cos, 1.0)
    s_first = jnp.where(first, -sin, 0.0)
    s_second = jnp.where(second, sin, 0.0)

    def rope(x):
        return (x * c_tab + pltpu.roll(x, HEAD_PAD - ROPE_HALF, 1) * s_first
                + pltpu.roll(x, ROPE_HALF, 1) * s_second)

    scale = MLA_QK ** -0.5
    qn = _rms(qc_ref[...].astype(F32), qg_ref[...]).astype(BF16)
    kvn = _rms(kvc_ref[...].astype(F32), kvg_ref[...]).astype(BF16)
    kr = rope(kr_ref[...])
    for hp in range(MLA_HEADS // 2):
        cols2 = slice(2 * hp * HEAD_PAD, (2 * hp + 2) * HEAD_PAD)
        q2 = jnp.dot(qn, wuq_ref[:, cols2], preferred_element_type=F32)
        k2 = jnp.dot(kvn, wuk_ref[:, cols2], preferred_element_type=F32)
        for hh in range(2):
            cols = slice((2 * hp + hh) * HEAD_PAD, (2 * hp + hh + 1) * HEAD_PAD)
            half = slice(hh * HEAD_PAD, (hh + 1) * HEAD_PAD)
            q_out[:, cols] = (rope(q2[:, half]) * scale).astype(BF16)
            k_out[:, cols] = (k2[:, half] + kr).astype(BF16)
    v_out[...] = jnp.dot(kvn, wuv_ref[...], preferred_element_type=F32).astype(BF16)


def _mla_prep(qc, kvc, kr, pos, inv_lane, q_g, kv_g, wuq_pad, wuk_pad, wuv):
    t = qc.shape[0]

    def tok(width):
        return pl.BlockSpec((TM, width), lambda i: (i, 0))

    wide = MLA_HEADS * HEAD_PAD
    return pl.pallas_call(
        _mla_prep_kernel,
        grid=(t // TM,),
        in_specs=[tok(MLA_Q_RANK), tok(MLA_KV_RANK), tok(HEAD_PAD), tok(1), _resident((1, HEAD_PAD)),
                  _resident((1, MLA_Q_RANK)), _resident((1, MLA_KV_RANK)),
                  _resident((MLA_Q_RANK, wide)), _resident((MLA_KV_RANK, wide)),
                  _resident((MLA_KV_RANK, MLA_HEADS * MLA_V))],
        out_specs=[tok(wide), tok(wide), tok(MLA_HEADS * MLA_V)],
        out_shape=[jax.ShapeDtypeStruct((t, wide), BF16), jax.ShapeDtypeStruct((t, wide), BF16),
                   jax.ShapeDtypeStruct((t, MLA_HEADS * MLA_V), BF16)],
        compiler_params=_params("parallel"),
        name="mla_prep",
    )(qc, kvc, kr, pos, inv_lane, q_g, kv_g, wuq_pad, wuk_pad, wuv)


def _attn_kernel(q_ref, k_ref, v_ref, o_ref):
    i = pl.program_id(2)
    causal = (lax.broadcasted_iota(jnp.int32, (TQ, TQ), 1) <= lax.broadcasted_iota(jnp.int32, (TQ, TQ), 0))
    outs = []
    for hh in range(2):
        hcols = slice(hh * HEAD_PAD, (hh + 1) * HEAD_PAD)
        q = q_ref[:, hcols]

        def block(jb, carry, masked):
            m, l, acc = carry
            rows = pl.ds(pl.multiple_of(jb * TQ, TQ), TQ)
            s = lax.dot_general(q, k_ref[rows, hcols], (((1,), (1,)), ((), ())), preferred_element_type=F32)
            if masked:
                s = jnp.where(causal, s, -jnp.inf)
            m_new = jnp.maximum(m, jnp.max(s, axis=1, keepdims=True))
            alpha = jnp.exp(m - m_new)
            p = jnp.exp(s - m_new)
            l = alpha * l + jnp.sum(p, axis=1, keepdims=True)
            acc = alpha * acc + jnp.dot(p.astype(BF16), v_ref[rows, :], preferred_element_type=F32)
            return m_new, l, acc

        init = (jnp.full((TQ, 1), -jnp.inf, F32), jnp.zeros((TQ, 1), F32), jnp.zeros((TQ, 2 * MLA_V), F32))
        carry = lax.fori_loop(0, i, functools.partial(block, masked=False), init)
        _, l, acc = block(i, carry, True)
        outs.append(acc / l)
    lane = lax.broadcasted_iota(jnp.int32, (TQ, 2 * MLA_V), 1)
    o_ref[...] = jnp.where(lane < MLA_V, outs[0], outs[1]).astype(BF16)


def _attention(qf, kf, v, batch, seq):
    t = qf.shape[0]
    nq = seq // TQ
    return pl.pallas_call(
        _attn_kernel,
        grid=(batch, MLA_HEADS // 2, nq),
        in_specs=[pl.BlockSpec((TQ, 2 * HEAD_PAD), lambda b, hp, i: (b * nq + i, hp)),
                  pl.BlockSpec((seq, 2 * HEAD_PAD), lambda b, hp, i: (b, hp)),
                  pl.BlockSpec((seq, 2 * MLA_V), lambda b, hp, i: (b, hp))],
        out_specs=pl.BlockSpec((TQ, 2 * MLA_V), lambda b, hp, i: (b * nq + i, hp)),
        out_shape=jax.ShapeDtypeStruct((t, MLA_HEADS * MLA_V), BF16),
        compiler_params=_params("parallel", "parallel", "arbitrary"),
        name="mla_attention",
    )(qf, kf, v)


def _merge_kernel(x_ref, yn_ref, o_ref, gate_ref, gb_ref, wssd_ref, wmla_ref, wout_ref, post_g_ref, out_ref):
    y_ssd = jnp.dot(yn_ref[...], wssd_ref[...], preferred_element_type=F32)
    y_mla = jnp.dot(o_ref[...], wmla_ref[...], preferred_element_type=F32)
    gates = jax.nn.sigmoid(gate_ref[...].astype(F32) + gb_ref[...])
    mixed = gates[:, :D_MODEL] * y_ssd + gates[:, D_MODEL:] * y_mla
    h = jnp.dot(mixed.astype(BF16), wout_ref[...], preferred_element_type=F32)
    out_ref[...] = x_ref[...] + _rms(h, post_g_ref[...])


def _merge(x, yn, o, gate, gate_bias, wssd, wmla, wout, post_g):
    t = x.shape[0]

    def tok(width):
        return pl.BlockSpec((TM, width), lambda i: (i, 0))

    sq = _resident((D_MODEL, D_MODEL))
    return pl.pallas_call(
        _merge_kernel,
        grid=(t // TM,),
        in_specs=[tok(D_MODEL), tok(SSD_INNER), tok(MLA_HEADS * MLA_V), tok(2 * D_MODEL),
                  _resident((1, 2 * D_MODEL)), sq, sq, sq, _resident((1, D_MODEL))],
        out_specs=tok(D_MODEL),
        out_shape=jax.ShapeDtypeStruct((t, D_MODEL), F32),
        compiler_params=_params("parallel"),
        name="merge",
    )(x, yn, o, gate, gate_bias, wssd, wmla, wout, post_g)


def _memkv_kernel(mem_ref, g_ref, wk_ref, wv_ref, k_ref, v_ref):
    mn = _rms(mem_ref[...], g_ref[...]).astype(BF16)
    k_ref[...] = jnp.dot(mn, wk_ref[...], preferred_element_type=F32).astype(BF16)
    v_ref[...] = jnp.dot(mn, wv_ref[...], preferred_element_type=F32).astype(BF16)


def _memkv(mem, g, wk, wv, mem_len):
    t = mem.shape[0]
    blk = pl.BlockSpec((mem_len, D_MODEL), lambda b: (b, 0))
    sq = _resident((D_MODEL, D_MODEL))
    return pl.pallas_call(
        _memkv_kernel,
        grid=(t // mem_len,),
        in_specs=[blk, _resident((1, D_MODEL)), sq, sq],
        out_specs=[blk, blk],
        out_shape=[jax.ShapeDtypeStruct((t, D_MODEL), BF16)] * 2,
        compiler_params=_params("parallel"),
        name="memkv",
    )(mem, g, wk, wv)


def _xattn_kernel(x_ref, g_ref, wq_ref, k_ref, v_ref, wo_ref, post_g_ref, out_ref):
    x = x_ref[...]
    hn = _rms(x, g_ref[...]).astype(BF16)
    q = (jnp.dot(hn, wq_ref[...], preferred_element_type=F32) * (XA_HEAD_DIM ** -0.5)).astype(BF16)
    outs = []
    for h in range(XA_HEADS):
        cols = slice(h * XA_HEAD_DIM, (h + 1) * XA_HEAD_DIM)
        s = lax.dot_general(q[:, cols], k_ref[:, cols], (((1,), (1,)), ((), ())), preferred_element_type=F32)
        p = jnp.exp(s - jnp.max(s, axis=1, keepdims=True))
        l = jnp.sum(p, axis=1, keepdims=True)
        o = jnp.dot(p.astype(BF16), v_ref[:, cols], preferred_element_type=F32) / l
        outs.append(o.astype(BF16))
    o = jnp.concatenate(outs, axis=1)
    y = jnp.dot(o, wo_ref[...], preferred_element_type=F32)
    out_ref[...] = x + _rms(y, post_g_ref[...])


def _xattn(x, g, wq, k, v, wo, post_g, batch, seq, mem_len):
    t = x.shape[0]
    nblk = seq // TM
    tok = pl.BlockSpec((TM, D_MODEL), lambda b, j: (b * nblk + j, 0))
    kv = pl.BlockSpec((mem_len, D_MODEL), lambda b, j: (b, 0))
    sq = _resident((D_MODEL, D_MODEL))
    return pl.pallas_call(
        _xattn_kernel,
        grid=(batch, nblk),
        in_specs=[tok, _resident((1, D_MODEL)), sq, kv, kv, sq, _resident((1, D_MODEL))],
        out_specs=tok,
        out_shape=jax.ShapeDtypeStruct((t, D_MODEL), F32),
        compiler_params=_params("parallel", "arbitrary"),
        name="xattn",
    )(x, g, wq, k, v, wo, post_g)


def _row(v):
    return v.reshape(1, -1).astype(F32)


def _pad_cols(w, left, right):
    return jnp.pad(w, ((0, 0), (left, right)))


def _layer(x, mem, pos, l, p, batch, seq, mem_len):
    bf = lambda w: w.astype(BF16)

    x = _ffn(x, _row(p["ffn1_pre_g"][l]), bf(p["ffn1_w_gate"][l]), bf(p["ffn1_w_up"][l]),
             bf(p["ffn1_w_down"][l]), _row(p["ffn1_post_g"][l]))

    w_in = p["w_in"][l]
    o_z = 0
    o_xbc = o_z + SSD_INNER
    o_dt = o_xbc + SSD_CONV_CH
    o_qc = o_dt + SSD_HEADS
    o_kvc = o_qc + MLA_Q_RANK
    o_kr = o_kvc + MLA_KV_RANK
    o_gate = o_kr + MLA_ROPE
    w_arr = jnp.concatenate([
        w_in[:, o_z:o_xbc], w_in[:, o_xbc:o_dt], w_in[:, o_gate:], w_in[:, o_qc:o_kvc], w_in[:, o_kvc:o_kr],
        _pad_cols(w_in[:, o_kr:o_gate], MLA_NOPE, HEAD_PAD - MLA_QK),
        _pad_cols(w_in[:, o_dt:o_qc], 0, LANES - SSD_HEADS)], axis=1)
    z, xbc, gate, qc, kvc, kr, dt = _inproj(x, _row(p["mix_pre_g"][l]), bf(w_arr))

    d_skip_exp = jnp.repeat(p["d_skip"][l].astype(F32), SSD_HEAD_DIM).reshape(1, SSD_INNER)
    pad_heads = lambda v: jnp.pad(v.astype(F32), (0, LANES - SSD_HEADS)).reshape(1, LANES)
    yn = _ssd(xbc, z, dt, p["conv_w"][l].astype(F32), _row(p["conv_b"][l]), pad_heads(p["dt_bias"][l]),
              pad_heads(p["a_log"][l]), d_skip_exp, _row(p["ssd_norm_g"][l]), batch, seq)

    inv = ROPE_THETA ** (-jnp.arange(0, MLA_ROPE, 2, dtype=F32) / MLA_ROPE)
    inv_lane = jnp.concatenate([jnp.zeros((MLA_NOPE,), F32), inv, inv,
                                jnp.zeros((HEAD_PAD - MLA_QK,), F32)]).reshape(1, HEAD_PAD)
    wuq_pad = jnp.pad(p["w_uq"][l].reshape(MLA_Q_RANK, MLA_HEADS, MLA_QK),
                      ((0, 0), (0, 0), (0, HEAD_PAD - MLA_QK))).reshape(MLA_Q_RANK, MLA_HEADS * HEAD_PAD)
    wuk_pad = jnp.pad(p["w_uk"][l].reshape(MLA_KV_RANK, MLA_HEADS, MLA_NOPE),
                      ((0, 0), (0, 0), (0, HEAD_PAD - MLA_NOPE))).reshape(MLA_KV_RANK, MLA_HEADS * HEAD_PAD)
    qf, kf, v = _mla_prep(qc, kvc, kr, pos, inv_lane, _row(p["q_norm_g"][l]), _row(p["kv_norm_g"][l]),
                          bf(wuq_pad), bf(wuk_pad), bf(p["w_uv"][l]))
    o = _attention(qf, kf, v, batch, seq)

    x = _merge(x, yn, o, gate, _row(p["gate_bias"][l]), bf(p["w_ssd_proj"][l]), bf(p["w_mla_proj"][l]),
               bf(p["w_out"][l]), _row(p["mix_post_g"][l]))

    mk, mv = _memkv(mem, _row(p["mem_norm_g"][l]), bf(p["w_xk"][l]), bf(p["w_xv"][l]), mem_len)
    x = _xattn(x, _row(p["xa_pre_g"][l]), bf(p["w_xq"][l]), mk, mv, bf(p["w_xo"][l]), _row(p["xa_post_g"][l]),
               batch, seq, mem_len)

    return _ffn(x, _row(p["ffn2_pre_g"][l]), bf(p["ffn2_w_gate"][l]), bf(p["ffn2_w_up"][l]),
                bf(p["ffn2_w_down"][l]), _row(p["ffn2_post_g"][l]))


def kernel(x, mem, positions, ffn1_pre_g, ffn1_w_gate, ffn1_w_up, ffn1_w_down, ffn1_post_g, mix_pre_g, w_in, conv_w, conv_b, dt_bias, a_log, d_skip, ssd_norm_g, w_ssd_proj, q_norm_g, w_uq, kv_norm_g, w_uk, w_uv, w_mla_proj, gate_bias, w_out, mix_post_g, xa_pre_g, mem_norm_g, w_xq, w_xk, w_xv, w_xo, xa_post_g, ffn2_pre_g, ffn2_w_gate, ffn2_w_up, ffn2_w_down, ffn2_post_g):
    p = dict(ffn1_pre_g=ffn1_pre_g, ffn1_w_gate=ffn1_w_gate, ffn1_w_up=ffn1_w_up, ffn1_w_down=ffn1_w_down,
             ffn1_post_g=ffn1_post_g, mix_pre_g=mix_pre_g, w_in=w_in, conv_w=conv_w, conv_b=conv_b,
             dt_bias=dt_bias, a_log=a_log, d_skip=d_skip, ssd_norm_g=ssd_norm_g, w_ssd_proj=w_ssd_proj,
             q_norm_g=q_norm_g, w_uq=w_uq, kv_norm_g=kv_norm_g, w_uk=w_uk, w_uv=w_uv, w_mla_proj=w_mla_proj,
             gate_bias=gate_bias, w_out=w_out, mix_post_g=mix_post_g, xa_pre_g=xa_pre_g, mem_norm_g=mem_norm_g,
             w_xq=w_xq, w_xk=w_xk, w_xv=w_xv, w_xo=w_xo, xa_post_g=xa_post_g, ffn2_pre_g=ffn2_pre_g,
             ffn2_w_gate=ffn2_w_gate, ffn2_w_up=ffn2_w_up, ffn2_w_down=ffn2_w_down, ffn2_post_g=ffn2_post_g)
    batch, seq, _ = x.shape
    mem_len = mem.shape[1]
    assert seq % TS == 0 and seq % TM == 0 and seq % TQ == 0 and TS % SSD_CHUNK == 0
    xf = x.reshape(batch * seq, D_MODEL)
    memf = mem.reshape(batch * mem_len, D_MODEL)
    pos = positions.reshape(batch * seq, 1)
    for l in range(w_in.shape[0]):
        xf = _layer(xf, memf, pos, l, p, batch, seq, mem_len)
    return xf.reshape(batch, seq, D_MODEL)
```

```python
import functools

import jax
import jax.numpy as jnp
from jax import lax
from jax.experimental import pallas as pl
from jax.experimental.pallas import tpu as pltpu

F32 = jnp.float32
BF16 = jnp.bfloat16

D_MODEL = 1024
SSD_HEADS = 16
SSD_HEAD_DIM = 64
SSD_INNER = SSD_HEADS * SSD_HEAD_DIM
SSD_GROUPS = 2
SSD_STATE = 128
SSD_CONV = 4
SSD_CHUNK = 128
SSD_CONV_CH = SSD_INNER + 2 * SSD_GROUPS * SSD_STATE
MLA_HEADS = 16
MLA_Q_RANK = 384
MLA_KV_RANK = 256
MLA_NOPE = 64
MLA_ROPE = 32
MLA_V = 64
MLA_QK = MLA_NOPE + MLA_ROPE
ROPE_THETA = 10000.0
XA_HEADS = 4
XA_HEAD_DIM = D_MODEL // XA_HEADS
D_FF = 2816
FFN_RES_WEIGHT = 0.5
EPS = 1e-6

LANES = 128
HEAD_PAD = LANES
ROPE_HALF = MLA_ROPE // 2
GROUP_W = SSD_INNER // SSD_GROUPS
PAIRS_PER_GROUP = GROUP_W // LANES

TM = 512
FFN_CHUNK = 256
TS = 512
TQ = 256
CONV_PAD = 8
VMEM_LIMIT = 56 * 1024 * 1024

_C_Z = (0, SSD_INNER)
_C_XBC = (_C_Z[1], _C_Z[1] + SSD_CONV_CH)
_C_GATE = (_C_XBC[1], _C_XBC[1] + 2 * D_MODEL)
_C_QC = (_C_GATE[1], _C_GATE[1] + MLA_Q_RANK)
_C_KVC = (_C_QC[1], _C_QC[1] + MLA_KV_RANK)
_C_KR = (_C_KVC[1], _C_KVC[1] + HEAD_PAD)
_C_DT = (_C_KR[1], _C_KR[1] + LANES)
D_IN_ARR = _C_DT[1]


def _rms(x, g):
    return x * lax.rsqrt(jnp.mean(x * x, axis=-1, keepdims=True) + EPS) * g


def _softplus(x):
    return jnp.maximum(x, 0.0) + jnp.log1p(jnp.exp(-jnp.abs(x)))


def _resident(shape):
    zeros = (0,) * len(shape)
    return pl.BlockSpec(shape, lambda *_: zeros, pipeline_mode=pl.Buffered(1))


def _params(*sem):
    return pltpu.CompilerParams(dimension_semantics=sem, vmem_limit_bytes=VMEM_LIMIT)


def _ffn_kernel(x_ref, pre_g_ref, wg_ref, wu_ref, wd_ref, post_g_ref, o_ref, h_ref):
    x = x_ref[...]
    xn = _rms(x, pre_g_ref[...]).astype(BF16)
    for c in range(D_FF // FFN_CHUNK):
        sl = slice(c * FFN_CHUNK, (c + 1) * FFN_CHUNK)
        g = jnp.dot(xn, wg_ref[:, sl], preferred_element_type=F32)
        u = jnp.dot(xn, wu_ref[:, sl], preferred_element_type=F32)
        h_ref[:, sl] = (jax.nn.silu(g) * u).astype(BF16)
    y = jnp.dot(h_ref[...], wd_ref[...], preferred_element_type=F32)
    o_ref[...] = x + FFN_RES_WEIGHT * _rms(y, post_g_ref[...])


def _ffn(x, pre_g, wg, wu, wd, post_g):
    t = x.shape[0]
    tok = pl.BlockSpec((TM, D_MODEL), lambda i: (i, 0))
    return pl.pallas_call(
        _ffn_kernel,
        grid=(t // TM,),
        in_specs=[tok, _resident((1, D_MODEL)), _resident((D_MODEL, D_FF)), _resident((D_MODEL, D_FF)),
                  _resident((D_FF, D_MODEL)), _resident((1, D_MODEL))],
        out_specs=tok,
        out_shape=jax.ShapeDtypeStruct((t, D_MODEL), F32),
        scratch_shapes=[pltpu.VMEM((TM, D_FF), BF16)],
        compiler_params=_params("parallel"),
        name="ffn",
    )(x, pre_g, wg, wu, wd, post_g)


def _inproj_kernel(x_ref, g_ref, w_ref, z_ref, xbc_ref, gate_ref, qc_ref, kvc_ref, kr_ref, dt_ref):
    hn = _rms(x_ref[...], g_ref[...]).astype(BF16)

    def proj(cols):
        return jnp.dot(hn, w_ref[:, cols[0]:cols[1]], preferred_element_type=F32)

    z_ref[...] = proj(_C_Z).astype(BF16)
    xbc_ref[...] = proj(_C_XBC).astype(BF16)
    gate_ref[...] = proj(_C_GATE).astype(BF16)
    qc_ref[...] = proj(_C_QC).astype(BF16)
    kvc_ref[...] = proj(_C_KVC).astype(BF16)
    kr_ref[...] = proj(_C_KR)
    dt_ref[...] = proj(_C_DT)


def _inproj(x, g, w_arr):
    t = x.shape[0]

    def tok(width):
        return pl.BlockSpec((TM, width), lambda i: (i, 0))

    widths = (SSD_INNER, SSD_CONV_CH, 2 * D_MODEL, MLA_Q_RANK, MLA_KV_RANK, HEAD_PAD, LANES)
    dtypes = (BF16, BF16, BF16, BF16, BF16, F32, F32)
    return pl.pallas_call(
        _inproj_kernel,
        grid=(t // TM,),
        in_specs=[tok(D_MODEL), _resident((1, D_MODEL)), _resident((D_MODEL, D_IN_ARR))],
        out_specs=[tok(w) for w in widths],
        out_shape=[jax.ShapeDtypeStruct((t, w), d) for w, d in zip(widths, dtypes)],
        compiler_params=_params("parallel"),
        name="inproj",
    )(x, g, w_arr)


def _ssd_kernel(xbc_ref, z_ref, dt_ref, cw_ref, cb_ref, dtb_ref, alog_ref, dskip_ref, ng_ref,
                y_ref, xpad, u_scr, state):
    L = SSD_CHUNK

    @pl.when(pl.program_id(1) == 0)
    def _start_of_sequence():
        xpad[0:CONV_PAD, :] = jnp.zeros((CONV_PAD, SSD_CONV_CH), F32)
        state[...] = jnp.zeros_like(state)

    xpad[CONV_PAD:CONV_PAD + TS, :] = xbc_ref[...].astype(F32)
    u_all = cb_ref[...]
    for k in range(SSD_CONV):
        first_row = CONV_PAD - (SSD_CONV - 1) + k
        u_all = u_all + cw_ref[k:k + 1, :] * xpad[first_row:first_row + TS, :]
    u_scr[...] = jax.nn.silu(u_all)
    xpad[0:CONV_PAD, :] = xpad[TS:TS + CONV_PAD, :]

    lane = lax.broadcasted_iota(jnp.int32, (L, LANES), 1)
    low_half = lane < SSD_HEAD_DIM
    tril = lax.broadcasted_iota(jnp.int32, (L, L), 1) <= lax.broadcasted_iota(jnp.int32, (L, L), 0)
    tri_f = tril.astype(F32)
    a_neg = -jnp.exp(alog_ref[...])

    def chunk(c, carry):
        r0 = pl.multiple_of(c * L, L)
        u = u_scr[pl.ds(r0, L), :]
        xs = u[:, :SSD_INNER]
        bmat = u[:, SSD_INNER:SSD_INNER + SSD_GROUPS * SSD_STATE]
        cmat = u[:, SSD_INNER + SSD_GROUPS * SSD_STATE:]
        dt = _softplus(dt_ref[pl.ds(r0, L), :] + dtb_ref[...])
        adt = jnp.where(lane < SSD_HEADS, dt * a_neg, 0.0)
        cs = jnp.dot(tri_f, adt, precision=lax.Precision.HIGHEST, preferred_element_type=F32)
        cs_t = cs.T
        z = z_ref[pl.ds(r0, L), :].astype(F32)

        for g in range(SSD_GROUPS):
            bm_g = bmat[:, g * SSD_STATE:(g + 1) * SSD_STATE]
            cm_g = cmat[:, g * SSD_STATE:(g + 1) * SSD_STATE].astype(BF16)
            cbm = lax.dot_general(cm_g, bm_g.astype(BF16), (((1,), (1,)), ((), ())),
                                  preferred_element_type=F32)
            s_g = state[g]
            y_off = jnp.dot(cm_g, s_g.astype(BF16), preferred_element_type=F32)
            xw_parts, cd_parts, y_parts = [], [], []
            for pp in range(PAIRS_PER_GROUP):
                p = g * PAIRS_PER_GROUP + pp
                h0, h1 = 2 * p, 2 * p + 1
                cols = slice(p * LANES, (p + 1) * LANES)

                def pair(v):
                    return jnp.where(low_half, v[:, h0:h0 + 1], v[:, h1:h1 + 1])

                def decay(h):
                    return jnp.exp(jnp.where(tril, cs[:, h:h + 1] - cs_t[h:h + 1, :], -jnp.inf))

                dt_p = pair(dt)
                cs_p = pair(cs)
                xs_p = xs[:, cols]
                xdt_p = xs_p * dt_p
                ecs_p = jnp.exp(cs_p)
                cs_last = cs_p[L - 1:L, :]
                xw_parts.append((xdt_p * jnp.exp(cs_last - cs_p)).astype(BF16))
                cd_parts.append(ecs_p[L - 1:L, :])
                m2 = jnp.concatenate([(cbm * decay(h0)).astype(BF16), (cbm * decay(h1)).astype(BF16)], axis=1)
                x_blk = jnp.concatenate([jnp.where(low_half, xdt_p, 0.0), jnp.where(low_half, 0.0, xdt_p)],
                                        axis=0).astype(BF16)
                y_p = jnp.dot(m2, x_blk, preferred_element_type=F32)
                y_p = y_p + y_off[:, pp * LANES:(pp + 1) * LANES] * ecs_p + dskip_ref[:, cols] * xs_p
                y_parts.append(y_p * jax.nn.silu(z[:, cols]))
            xw_g = jnp.concatenate(xw_parts, axis=1)
            cd_g = jnp.concatenate(cd_parts, axis=1)
            state[g] = s_g * cd_g + jnp.dot(bm_g.T.astype(BF16), xw_g, preferred_element_type=F32)
            y_g = jnp.concatenate(y_parts, axis=1)
            gcols = slice(g * GROUP_W, (g + 1) * GROUP_W)
            y_ref[pl.ds(r0, L), gcols] = _rms(y_g, ng_ref[:, gcols]).astype(BF16)
        return carry

    lax.fori_loop(0, TS // L, chunk, 0)


def _ssd(xbc, z, dt, conv_w, conv_b, dt_bias_pad, a_log_pad, d_skip_exp, norm_g, batch, seq):
    t = xbc.shape[0]
    nblk = seq // TS

    def tok(width):
        return pl.BlockSpec((TS, width), lambda b, j: (b * nblk + j, 0))

    return pl.pallas_call(
        _ssd_kernel,
        grid=(batch, nblk),
        in_specs=[tok(SSD_CONV_CH), tok(SSD_INNER), tok(LANES),
                  _resident((SSD_CONV, SSD_CONV_CH)), _resident((1, SSD_CONV_CH)),
                  _resident((1, LANES)), _resident((1, LANES)),
                  _resident((1, SSD_INNER)), _resident((1, SSD_INNER))],
        out_specs=tok(SSD_INNER),
        out_shape=jax.ShapeDtypeStruct((t, SSD_INNER), BF16),
        scratch_shapes=[pltpu.VMEM((TS + CONV_PAD, SSD_CONV_CH), F32),
                        pltpu.VMEM((TS, SSD_CONV_CH), F32),
                        pltpu.VMEM((SSD_GROUPS, SSD_STATE, GROUP_W), F32)],
        compiler_params=_params("parallel", "arbitrary"),
        name="ssd",
    )(xbc, z, dt, conv_w, conv_b, dt_bias_pad, a_log_pad, d_skip_exp, norm_g)


def _mla_prep_kernel(qc_ref, kvc_ref, kr_ref, pos_ref, inv_ref, qg_ref, kvg_ref, wuq_ref, wuk_ref, wuv_ref,
                     q_out, k_out, v_out):
    ang = pos_ref[...].astype(F32) * inv_ref[...]
    cos = jnp.cos(ang)
    sin = jnp.sin(ang)
    lane = lax.broadcasted_iota(jnp.int32, ang.shape, 1)
    first = (lane >= MLA_NOPE) & (lane < MLA_NOPE + ROPE_HALF)
    second = (lane >= MLA_NOPE + ROPE_HALF) & (lane < MLA_QK)
    c_tab = jnp.where(first | second, cos, 1.0)
    s_first = jnp.where(first, -sin, 0.0)
    s_second = jnp.where(second, sin, 0.0)

    def rope(x):
        return (x * c_tab + pltpu.roll(x, HEAD_PAD - ROPE_HALF, 1) * s_first
                + pltpu.roll(x, ROPE_HALF, 1) * s_second)

    scale = MLA_QK ** -0.5
    qn = _rms(qc_ref[...].astype(F32), qg_ref[...]).astype(BF16)
    kvn = _rms(kvc_ref[...].astype(F32), kvg_ref[...]).astype(BF16)
    kr = rope(kr_ref[...])
    for hp in range(MLA_HEADS // 2):
        cols2 = slice(2 * hp * HEAD_PAD, (2 * hp + 2) * HEAD_PAD)
        q2 = jnp.dot(qn, wuq_ref[:, cols2], preferred_element_type=F32)
        k2 = jnp.dot(kvn, wuk_ref[:, cols2], preferred_element_type=F32)
        for hh in range(2):
            cols = slice((2 * hp + hh) * HEAD_PAD, (2 * hp + hh + 1) * HEAD_PAD)
            half = slice(hh * HEAD_PAD, (hh + 1) * HEAD_PAD)
            q_out[:, cols] = (rope(q2[:, half]) * scale).astype(BF16)
            k_out[:, cols] = (k2[:, half] + kr).astype(BF16)
    v_out[...] = jnp.dot(kvn, wuv_ref[...], preferred_element_type=F32).astype(BF16)


def _mla_prep(qc, kvc, kr, pos, inv_lane, q_g, kv_g, wuq_pad, wuk_pad, wuv):
    t = qc.shape[0]

    def tok(width):
        return pl.BlockSpec((TM, width), lambda i: (i, 0))

    wide = MLA_HEADS * HEAD_PAD
    return pl.pallas_call(
        _mla_prep_kernel,
        grid=(t // TM,),
        in_specs=[tok(MLA_Q_RANK), tok(MLA_KV_RANK), tok(HEAD_PAD), tok(1), _resident((1, HEAD_PAD)),
                  _resident((1, MLA_Q_RANK)), _resident((1, MLA_KV_RANK)),
                  _resident((MLA_Q_RANK, wide)), _resident((MLA_KV_RANK, wide)),
                  _resident((MLA_KV_RANK, MLA_HEADS * MLA_V))],
        out_specs=[tok(wide), tok(wide), tok(MLA_HEADS * MLA_V)],
        out_shape=[jax.ShapeDtypeStruct((t, wide), BF16), jax.ShapeDtypeStruct((t, wide), BF16),
                   jax.ShapeDtypeStruct((t, MLA_HEADS * MLA_V), BF16)],
        compiler_params=_params("parallel"),
        name="mla_prep",
    )(qc, kvc, kr, pos, inv_lane, q_g, kv_g, wuq_pad, wuk_pad, wuv)


def _attn_kernel(q_ref, k_ref, v_ref, o_ref):
    seq = q_ref.shape[0]
    causal = (lax.broadcasted_iota(jnp.int32, (TQ, TQ), 1) <= lax.broadcasted_iota(jnp.int32, (TQ, TQ), 0))
    lane = lax.broadcasted_iota(jnp.int32, (TQ, 2 * MLA_V), 1)
    nt = (((1,), (1,)), ((), ()))
    for i in range(seq // TQ):
        rows = slice(i * TQ, (i + 1) * TQ)
        past = i * TQ
        outs = []
        for hh in range(2):
            hcols = slice(hh * HEAD_PAD, (hh + 1) * HEAD_PAD)
            q = q_ref[rows, hcols]
            s_diag = jnp.where(causal, lax.dot_general(q, k_ref[rows, hcols], nt, preferred_element_type=F32),
                               -jnp.inf)
            m = jnp.max(s_diag, axis=1, keepdims=True)
            if past:
                s_past = lax.dot_general(q, k_ref[0:past, hcols], nt, preferred_element_type=F32)
                m = jnp.maximum(m, jnp.max(s_past, axis=1, keepdims=True))
            p_diag = jnp.exp(s_diag - m)
            l = jnp.sum(p_diag, axis=1, keepdims=True)
            acc = jnp.dot(p_diag.astype(BF16), v_ref[rows, :], preferred_element_type=F32)
            if past:
                p_past = jnp.exp(s_past - m)
                l = l + jnp.sum(p_past, axis=1, keepdims=True)
                acc = acc + jnp.dot(p_past.astype(BF16), v_ref[0:past, :], preferred_element_type=F32)
            outs.append(acc / l)
        o_ref[rows, :] = jnp.where(lane < MLA_V, outs[0], outs[1]).astype(BF16)


def _attention(qf, kf, v, batch, seq):
    t = qf.shape[0]
    return pl.pallas_call(
        _attn_kernel,
        grid=(batch, MLA_HEADS // 2),
        in_specs=[pl.BlockSpec((seq, 2 * HEAD_PAD), lambda b, hp: (b, hp)),
                  pl.BlockSpec((seq, 2 * HEAD_PAD), lambda b, hp: (b, hp)),
                  pl.BlockSpec((seq, 2 * MLA_V), lambda b, hp: (b, hp))],
        out_specs=pl.BlockSpec((seq, 2 * MLA_V), lambda b, hp: (b, hp)),
        out_shape=jax.ShapeDtypeStruct((t, MLA_HEADS * MLA_V), BF16),
        compiler_params=_params("parallel", "parallel"),
        name="mla_attention",
    )(qf, kf, v)


def _merge_kernel(x_ref, yn_ref, o_ref, gate_ref, gb_ref, wssd_ref, wmla_ref, wout_ref, post_g_ref, out_ref):
    y_ssd = jnp.dot(yn_ref[...], wssd_ref[...], preferred_element_type=F32)
    y_mla = jnp.dot(o_ref[...], wmla_ref[...], preferred_element_type=F32)
    gates = jax.nn.sigmoid(gate_ref[...].astype(F32) + gb_ref[...])
    mixed = gates[:, :D_MODEL] * y_ssd + gates[:, D_MODEL:] * y_mla
    h = jnp.dot(mixed.astype(BF16), wout_ref[...], preferred_element_type=F32)
    out_ref[...] = x_ref[...] + _rms(h, post_g_ref[...])


def _merge(x, yn, o, gate, gate_bias, wssd, wmla, wout, post_g):
    t = x.shape[0]

    def tok(width):
        return pl.BlockSpec((TM, width), lambda i: (i, 0))

    sq = _resident((D_MODEL, D_MODEL))
    return pl.pallas_call(
        _merge_kernel,
        grid=(t // TM,),
        in_specs=[tok(D_MODEL), tok(SSD_INNER), tok(MLA_HEADS * MLA_V), tok(2 * D_MODEL),
                  _resident((1, 2 * D_MODEL)), sq, sq, sq, _resident((1, D_MODEL))],
        out_specs=tok(D_MODEL),
        out_shape=jax.ShapeDtypeStruct((t, D_MODEL), F32),
        compiler_params=_params("parallel"),
        name="merge",
    )(x, yn, o, gate, gate_bias, wssd, wmla, wout, post_g)


def _memkv_kernel(mem_ref, g_ref, wk_ref, wv_ref, k_ref, v_ref):
    mn = _rms(mem_ref[...], g_ref[...]).astype(BF16)
    k_ref[...] = jnp.dot(mn, wk_ref[...], preferred_element_type=F32).astype(BF16)
    v_ref[...] = jnp.dot(mn, wv_ref[...], preferred_element_type=F32).astype(BF16)


def _memkv(mem, g, wk, wv, mem_len):
    t = mem.shape[0]
    blk = pl.BlockSpec((mem_len, D_MODEL), lambda b: (b, 0))
    sq = _resident((D_MODEL, D_MODEL))
    return pl.pallas_call(
        _memkv_kernel,
        grid=(t // mem_len,),
        in_specs=[blk, _resident((1, D_MODEL)), sq, sq],
        out_specs=[blk, blk],
        out_shape=[jax.ShapeDtypeStruct((t, D_MODEL), BF16)] * 2,
        compiler_params=_params("parallel"),
        name="memkv",
    )(mem, g, wk, wv)


def _xattn_kernel(x_ref, g_ref, wq_ref, k_ref, v_ref, wo_ref, post_g_ref, out_ref):
    x = x_ref[...]
    hn = _rms(x, g_ref[...]).astype(BF16)
    q = (jnp.dot(hn, wq_ref[...], preferred_element_type=F32) * (XA_HEAD_DIM ** -0.5)).astype(BF16)
    outs = []
    for h in range(XA_HEADS):
        cols = slice(h * XA_HEAD_DIM, (h + 1) * XA_HEAD_DIM)
        s = lax.dot_general(q[:, cols], k_ref[:, cols], (((1,), (1,)), ((), ())), preferred_element_type=F32)
        p = jnp.exp(s - jnp.max(s, axis=1, keepdims=True))
        l = jnp.sum(p, axis=1, keepdims=True)
        o = jnp.dot(p.astype(BF16), v_ref[:, cols], preferred_element_type=F32) / l
        outs.append(o.astype(BF16))
    o = jnp.concatenate(outs, axis=1)
    y = jnp.dot(o, wo_ref[...], preferred_element_type=F32)
    out_ref[...] = x + _rms(y, post_g_ref[...])


def _xattn(x, g, wq, k, v, wo, post_g, batch, seq, mem_len):
    t = x.shape[0]
    nblk = seq // TM
    tok = pl.BlockSpec((TM, D_MODEL), lambda b, j: (b * nblk + j, 0))
    kv = pl.BlockSpec((mem_len, D_MODEL), lambda b, j: (b, 0))
    sq = _resident((D_MODEL, D_MODEL))
    return pl.pallas_call(
        _xattn_kernel,
        grid=(batch, nblk),
        in_specs=[tok, _resident((1, D_MODEL)), sq, kv, kv, sq, _resident((1, D_MODEL))],
        out_specs=tok,
        out_shape=jax.ShapeDtypeStruct((t, D_MODEL), F32),
        compiler_params=_params("parallel", "arbitrary"),
        name="xattn",
    )(x, g, wq, k, v, wo, post_g)


def _row(v):
    return v.reshape(1, -1).astype(F32)


def _pad_cols(w, left, right):
    return jnp.pad(w, ((0, 0), (left, right)))


def _layer(x, mem, pos, l, p, batch, seq, mem_len):
    bf = lambda w: w.astype(BF16)

    x = _ffn(x, _row(p["ffn1_pre_g"][l]), bf(p["ffn1_w_gate"][l]), bf(p["ffn1_w_up"][l]),
             bf(p["ffn1_w_down"][l]), _row(p["ffn1_post_g"][l]))

    w_in = p["w_in"][l]
    o_z = 0
    o_xbc = o_z + SSD_INNER
    o_dt = o_xbc + SSD_CONV_CH
    o_qc = o_dt + SSD_HEADS
    o_kvc = o_qc + MLA_Q_RANK
    o_kr = o_kvc + MLA_KV_RANK
    o_gate = o_kr + MLA_ROPE
    w_arr = jnp.concatenate([
        w_in[:, o_z:o_xbc], w_in[:, o_xbc:o_dt], w_in[:, o_gate:], w_in[:, o_qc:o_kvc], w_in[:, o_kvc:o_kr],
        _pad_cols(w_in[:, o_kr:o_gate], MLA_NOPE, HEAD_PAD - MLA_QK),
        _pad_cols(w_in[:, o_dt:o_qc], 0, LANES - SSD_HEADS)], axis=1)
    z, xbc, gate, qc, kvc, kr, dt = _inproj(x, _row(p["mix_pre_g"][l]), bf(w_arr))

    d_skip_exp = jnp.repeat(p["d_skip"][l].astype(F32), SSD_HEAD_DIM).reshape(1, SSD_INNER)
    pad_heads = lambda v: jnp.pad(v.astype(F32), (0, LANES - SSD_HEADS)).reshape(1, LANES)
    yn = _ssd(xbc, z, dt, p["conv_w"][l].astype(F32), _row(p["conv_b"][l]), pad_heads(p["dt_bias"][l]),
              pad_heads(p["a_log"][l]), d_skip_exp, _row(p["ssd_norm_g"][l]), batch, seq)

    inv = ROPE_THETA ** (-jnp.arange(0, MLA_ROPE, 2, dtype=F32) / MLA_ROPE)
    inv_lane = jnp.concatenate([jnp.zeros((MLA_NOPE,), F32), inv, inv,
                                jnp.zeros((HEAD_PAD - MLA_QK,), F32)]).reshape(1, HEAD_PAD)
    wuq_pad = jnp.pad(p["w_uq"][l].reshape(MLA_Q_RANK, MLA_HEADS, MLA_QK),
                      ((0, 0), (0, 0), (0, HEAD_PAD - MLA_QK))).reshape(MLA_Q_RANK, MLA_HEADS * HEAD_PAD)
    wuk_pad = jnp.pad(p["w_uk"][l].reshape(MLA_KV_RANK, MLA_HEADS, MLA_NOPE),
                      ((0, 0), (0, 0), (0, HEAD_PAD - MLA_NOPE))).reshape(MLA_KV_RANK, MLA_HEADS * HEAD_PAD)
    qf, kf, v = _mla_prep(qc, kvc, kr, pos, inv_lane, _row(p["q_norm_g"][l]), _row(p["kv_norm_g"][l]),
                          bf(wuq_pad), bf(wuk_pad), bf(p["w_uv"][l]))
    o = _attention(qf, kf, v, batch, seq)

    x = _merge(x, yn, o, gate, _row(p["gate_bias"][l]), bf(p["w_ssd_proj"][l]), bf(p["w_mla_proj"][l]),
               bf(p["w_out"][l]), _row(p["mix_post_g"][l]))

    mk, mv = _memkv(mem, _row(p["mem_norm_g"][l]), bf(p["w_xk"][l]), bf(p["w_xv"][l]), mem_len)
    x = _xattn(x, _row(p["xa_pre_g"][l]), bf(p["w_xq"][l]), mk, mv, bf(p["w_xo"][l]), _row(p["xa_post_g"][l]),
               batch, seq, mem_len)

    return _ffn(x, _row(p["ffn2_pre_g"][l]), bf(p["ffn2_w_gate"][l]), bf(p["ffn2_w_up"][l]),
                bf(p["ffn2_w_down"][l]), _row(p["ffn2_post_g"][l]))


def kernel(x, mem, positions, ffn1_pre_g, ffn1_w_gate, ffn1_w_up, ffn1_w_down, ffn1_post_g, mix_pre_g, w_in, conv_w, conv_b, dt_bias, a_log, d_skip, ssd_norm_g, w_ssd_proj, q_norm_g, w_uq, kv_norm_g, w_uk, w_uv, w_mla_proj, gate_bias, w_out, mix_post_g, xa_pre_g, mem_norm_g, w_xq, w_xk, w_xv, w_xo, xa_post_g, ffn2_pre_g, ffn2_w_gate, ffn2_w_up, ffn2_w_down, ffn2_post_g):
    p = dict(ffn1_pre_g=ffn1_pre_g, ffn1_w_gate=ffn1_w_gate, ffn1_w_up=ffn1_w_up, ffn1_w_down=ffn1_w_down,
             ffn1_post_g=ffn1_post_g, mix_pre_g=mix_pre_g, w_in=w_in, conv_w=conv_w, conv_b=conv_b,
             dt_bias=dt_bias, a_log=a_log, d_skip=d_skip, ssd_norm_g=ssd_norm_g, w_ssd_proj=w_ssd_proj,
             q_norm_g=q_norm_g, w_uq=w_uq, kv_norm_g=kv_norm_g, w_uk=w_uk, w_uv=w_uv, w_mla_proj=w_mla_proj,
             gate_bias=gate_bias, w_out=w_out, mix_post_g=mix_post_g, xa_pre_g=xa_pre_g, mem_norm_g=mem_norm_g,
             w_xq=w_xq, w_xk=w_xk, w_xv=w_xv, w_xo=w_xo, xa_post_g=xa_post_g, ffn2_pre_g=ffn2_pre_g,
             ffn2_w_gate=ffn2_w_gate, ffn2_w_up=ffn2_w_up, ffn2_w_down=ffn2_w_down, ffn2_post_g=ffn2_post_g)
    batch, seq, _ = x.shape
    mem_len = mem.shape[1]
    assert seq % TS == 0 and seq % TM == 0 and seq % TQ == 0 and TS % SSD_CHUNK == 0
    xf = x.reshape(batch * seq, D_MODEL)
    memf = mem.reshape(batch * mem_len, D_MODEL)
    pos = positions.reshape(batch * seq, 1)
    for l in range(w_in.shape[0]):
        xf = _layer(xf, memf, pos, l, p, batch, seq, mem_len)
    return xf.reshape(batch, seq, D_MODEL)
```

```python
import math

import numpy as np
import jax
import jax.numpy as jnp
from jax import lax
from jax.experimental import pallas as pl
from jax.experimental.pallas import tpu as pltpu

F32 = jnp.float32
BF16 = jnp.bfloat16

D_MODEL = 1024
SSD_HEADS = 16
SSD_HEAD_DIM = 64
SSD_INNER = SSD_HEADS * SSD_HEAD_DIM
SSD_GROUPS = 2
SSD_STATE = 128
SSD_CONV = 4
SSD_CHUNK = 128
SSD_CONV_CH = SSD_INNER + 2 * SSD_GROUPS * SSD_STATE
MLA_HEADS = 16
MLA_Q_RANK = 384
MLA_KV_RANK = 256
MLA_NOPE = 64
MLA_ROPE = 32
MLA_V = 64
MLA_QK = MLA_NOPE + MLA_ROPE
ROPE_THETA = 10000.0
XA_HEADS = 4
XA_HEAD_DIM = D_MODEL // XA_HEADS
D_FF = 2816
FFN_RES_WEIGHT = 0.5
EPS = 1e-6

LANES = 128
BF16_ROWS = 16
HEAD_PAD = LANES
ROPE_HALF = MLA_ROPE // 2
GROUP_W = SSD_INNER // SSD_GROUPS
PAIRS_PER_GROUP = GROUP_W // LANES
LOG2E = math.log2(math.e)

TM = 512
FFN_CHUNK = 256
TS = 512
TQ = 256
HIST = BF16_ROWS
VMEM_LIMIT = 56 * 1024 * 1024

_C_Z = (0, SSD_INNER)
_C_XBC = (_C_Z[1], _C_Z[1] + SSD_CONV_CH)
_C_GATE = (_C_XBC[1], _C_XBC[1] + 2 * D_MODEL)
_C_QC = (_C_GATE[1], _C_GATE[1] + MLA_Q_RANK)
_C_KVC = (_C_QC[1], _C_QC[1] + MLA_KV_RANK)
_C_KR = (_C_KVC[1], _C_KVC[1] + HEAD_PAD)
_C_DT = (_C_KR[1], _C_KR[1] + LANES)
D_IN_ARR = _C_DT[1]

_ROWS_DT, _ROWS_ECS, _ROWS_DSF = (0, 32), (32, 64), (64, 96)
_ROWS_CS = (0, 48)


def _rms(x, g):
    return x * lax.rsqrt(jnp.mean(x * x, axis=-1, keepdims=True) + EPS) * g


def _softplus(x):
    return jnp.maximum(x, 0.0) + jnp.log1p(jnp.exp(-jnp.abs(x)))


def _bf16_pieces(x, n):
    pieces, rest = [], x
    for _ in range(n):
        piece = rest.astype(BF16).astype(F32)
        pieces.append(piece)
        rest = rest - piece
    return pieces


def _resident(shape):
    zeros = (0,) * len(shape)
    return pl.BlockSpec(shape, lambda *_: zeros, pipeline_mode=pl.Buffered(1))


def _params(*sem):
    return pltpu.CompilerParams(dimension_semantics=sem, vmem_limit_bytes=VMEM_LIMIT)


def _ffn_kernel(x_ref, pre_g_ref, wg_ref, wu_ref, wd_ref, post_g_ref, o_ref, h_ref):
    x = x_ref[...]
    xn = _rms(x, pre_g_ref[...]).astype(BF16)
    for c in range(D_FF // FFN_CHUNK):
        sl = slice(c * FFN_CHUNK, (c + 1) * FFN_CHUNK)
        g = jnp.dot(xn, wg_ref[:, sl], preferred_element_type=F32)
        u = jnp.dot(xn, wu_ref[:, sl], preferred_element_type=F32)
        h_ref[:, sl] = (jax.nn.silu(g) * u).astype(BF16)
    y = jnp.dot(h_ref[...], wd_ref[...], preferred_element_type=F32)
    o_ref[...] = x + FFN_RES_WEIGHT * _rms(y, post_g_ref[...])


def _ffn(x, pre_g, wg, wu, wd, post_g):
    t = x.shape[0]
    tok = pl.BlockSpec((TM, D_MODEL), lambda i: (i, 0))
    return pl.pallas_call(
        _ffn_kernel,
        grid=(t // TM,),
        in_specs=[tok, _resident((1, D_MODEL)), _resident((D_MODEL, D_FF)), _resident((D_MODEL, D_FF)),
                  _resident((D_FF, D_MODEL)), _resident((1, D_MODEL))],
        out_specs=tok,
        out_shape=jax.ShapeDtypeStruct((t, D_MODEL), F32),
        scratch_shapes=[pltpu.VMEM((TM, D_FF), BF16)],
        compiler_params=_params("parallel"),
        name="ffn",
    )(x, pre_g, wg, wu, wd, post_g)


def _inproj_kernel(x_ref, g_ref, w_ref, z_ref, xbc_ref, gate_ref, qc_ref, kvc_ref, kr_ref, dt_ref):
    hn = _rms(x_ref[...], g_ref[...]).astype(BF16)

    def proj(cols):
        return jnp.dot(hn, w_ref[:, cols[0]:cols[1]], preferred_element_type=F32)

    z_ref[...] = proj(_C_Z).astype(BF16)
    xbc_ref[...] = proj(_C_XBC).astype(BF16)
    gate_ref[...] = proj(_C_GATE).astype(BF16)
    qc_ref[...] = proj(_C_QC).astype(BF16)
    kvc_ref[...] = proj(_C_KVC).astype(BF16)
    kr_ref[...] = proj(_C_KR)
    dt_ref[...] = proj(_C_DT)


def _inproj(x, g, w_arr):
    t = x.shape[0]

    def tok(width):
        return pl.BlockSpec((TM, width), lambda i: (i, 0))

    widths = (SSD_INNER, SSD_CONV_CH, 2 * D_MODEL, MLA_Q_RANK, MLA_KV_RANK, HEAD_PAD, LANES)
    dtypes = (BF16, BF16, BF16, BF16, BF16, F32, F32)
    return pl.pallas_call(
        _inproj_kernel,
        grid=(t // TM,),
        in_specs=[tok(D_MODEL), _resident((1, D_MODEL)), _resident((D_MODEL, D_IN_ARR))],
        out_specs=[tok(w) for w in widths],
        out_shape=[jax.ShapeDtypeStruct((t, w), d) for w, d in zip(widths, dtypes)],
        compiler_params=_params("parallel"),
        name="inproj",
    )(x, g, w_arr)


def _ssd_constants():
    L = SSD_CHUNK
    shift = np.zeros(((SSD_CONV - 1) * L, L + HIST), np.float32)
    for k in range(SSD_CONV - 1):
        shift[k * L + np.arange(L), np.arange(L) + HIST - (SSD_CONV - 1) + k] = 1.0
    head_of_lane = np.arange(SSD_INNER) // SSD_HEAD_DIM
    exp_ch = np.zeros((LANES, 3 * SSD_INNER), np.float32)
    for q, (lo, hi) in enumerate((_ROWS_DT, _ROWS_ECS, _ROWS_DSF)):
        for r in range(lo, hi):
            exp_ch[r, q * SSD_INNER + np.nonzero(head_of_lane == r % SSD_HEADS)[0]] = 1.0
    exp_blk = np.zeros((LANES, SSD_HEADS * LANES), np.float32)
    for r in range(*_ROWS_CS):
        h = r % SSD_HEADS
        exp_blk[r, h * LANES:(h + 1) * LANES] = 1.0
    tri = np.tile((np.arange(L)[:, None] <= np.arange(L)[None, :]).astype(np.float32), (3, 1))
    return tuple(jnp.asarray(m, BF16) for m in (shift, exp_ch, exp_blk, tri))


def _ssd_kernel(xbc_ref, z_ref, dt_ref, shift_ref, expch_ref, expblk_ref, tri_ref, cw_ref, cb_ref, dtb_ref,
                alog_ref, dskip_ref, ng_ref, y_ref, xpad, state):
    L = SSD_CHUNK

    @pl.when(pl.program_id(1) == 0)
    def _start_of_sequence():
        xpad[0:HIST, :] = jnp.zeros((HIST, SSD_CONV_CH), BF16)
        state[...] = jnp.zeros_like(state)

    xpad[HIST:HIST + TS, :] = xbc_ref[...]

    low_half = lax.broadcasted_iota(jnp.int32, (L, LANES), 1) < SSD_HEAD_DIM
    tril = lax.broadcasted_iota(jnp.int32, (L, L), 1) <= lax.broadcasted_iota(jnp.int32, (L, L), 0)
    a_neg = -jnp.exp(alog_ref[...])

    def to_token_rows(stack):
        full = jnp.concatenate([stack, jnp.zeros((LANES - stack.shape[0], L), F32)], axis=0)
        return full.T.astype(BF16)

    def chunk(c, carry):
        r0 = pl.multiple_of(c * L, L)
        rows = pl.ds(r0, L)
        win = xpad[pl.ds(r0, L + HIST), :]
        taps = jnp.dot(shift_ref[...], win, preferred_element_type=F32)
        u = cb_ref[...] + cw_ref[SSD_CONV - 1:SSD_CONV, :] * win[HIST:, :].astype(F32)
        for k in range(SSD_CONV - 1):
            u = u + cw_ref[k:k + 1, :] * taps[k * L:(k + 1) * L, :]
        u = jax.nn.silu(u)
        xs = u[:, :SSD_INNER]
        bmat = u[:, SSD_INNER:SSD_INNER + SSD_GROUPS * SSD_STATE]
        cmat = u[:, SSD_INNER + SSD_GROUPS * SSD_STATE:]

        dt_t = _softplus(dt_ref[rows, :].T[:SSD_HEADS, :] + dtb_ref[...])
        adt_t = dt_t * a_neg
        cs_t = jnp.dot(jnp.concatenate(_bf16_pieces(adt_t, 3), axis=1).astype(BF16), tri_ref[...],
                       preferred_element_type=F32)
        ecs_t = jnp.exp(cs_t)
        dsf_t = jnp.exp(cs_t[:, L - 1:L] - cs_t)
        stack = jnp.concatenate(_bf16_pieces(dt_t, 2) + _bf16_pieces(ecs_t, 2) + _bf16_pieces(dsf_t, 2), axis=0)
        per_ch = jnp.dot(to_token_rows(stack), expch_ref[...], preferred_element_type=F32)
        dt_e = per_ch[:, :SSD_INNER]
        ecs_e = per_ch[:, SSD_INNER:2 * SSD_INNER]
        dsf_e = per_ch[:, 2 * SSD_INNER:]
        cs_col = jnp.dot(to_token_rows(jnp.concatenate(_bf16_pieces(cs_t, 3), axis=0)), expblk_ref[...],
                         preferred_element_type=F32)
        z = z_ref[rows, :].astype(F32)

        for g in range(SSD_GROUPS):
            gcols = slice(g * GROUP_W, (g + 1) * GROUP_W)
            bm_g = bmat[:, g * SSD_STATE:(g + 1) * SSD_STATE]
            cm_g = cmat[:, g * SSD_STATE:(g + 1) * SSD_STATE].astype(BF16)
            cbm = lax.dot_general(cm_g, bm_g.astype(BF16), (((1,), (1,)), ((), ())),
                                  preferred_element_type=F32)
            s_g = state[g]
            y_off = jnp.dot(cm_g, s_g.astype(BF16), preferred_element_type=F32)
            xs_g = xs[:, gcols]
            xdt_g = xs_g * dt_e[:, gcols]
            xw_g = (xdt_g * dsf_e[:, gcols]).astype(BF16)
            state[g] = (s_g * ecs_e[L - 1:L, gcols]
                        + jnp.dot(bm_g.T.astype(BF16), xw_g, preferred_element_type=F32))
            y_parts = []
            for pp in range(PAIRS_PER_GROUP):
                p = g * PAIRS_PER_GROUP + pp

                def masked(h):
                    dec = jnp.exp(jnp.where(tril, cs_col[:, h * LANES:(h + 1) * LANES] - cs_t[h:h + 1, :],
                                            -jnp.inf))
                    return (cbm * dec).astype(BF16)

                xdt_p = xdt_g[:, pp * LANES:(pp + 1) * LANES]
                m2 = jnp.concatenate([masked(2 * p), masked(2 * p + 1)], axis=1)
                x_blk = jnp.concatenate([jnp.where(low_half, xdt_p, 0.0), jnp.where(low_half, 0.0, xdt_p)],
                                        axis=0).astype(BF16)
                y_parts.append(jnp.dot(m2, x_blk, preferred_element_type=F32))
            y_g = (jnp.concatenate(y_parts, axis=1) + y_off * ecs_e[:, gcols] + dskip_ref[:, gcols] * xs_g)
            y_g = y_g * jax.nn.silu(z[:, gcols])
            y_ref[rows, gcols] = _rms(y_g, ng_ref[:, gcols]).astype(BF16)
        return carry

    lax.fori_loop(0, TS // L, chunk, 0)
    xpad[0:HIST, :] = xpad[TS:TS + HIST, :]


def _ssd(xbc, z, dt, conv_w, conv_b, dt_bias_t, a_log_t, d_skip_exp, norm_g, batch, seq):
    t = xbc.shape[0]
    nblk = seq // TS
    consts = _ssd_constants()

    def tok(width):
        return pl.BlockSpec((TS, width), lambda b, j: (b * nblk + j, 0))

    return pl.pallas_call(
        _ssd_kernel,
        grid=(batch, nblk),
        in_specs=[tok(SSD_CONV_CH), tok(SSD_INNER), tok(LANES)] + [_resident(m.shape) for m in consts] + [
            _resident((SSD_CONV, SSD_CONV_CH)), _resident((1, SSD_CONV_CH)),
            _resident((SSD_HEADS, SSD_CHUNK)), _resident((SSD_HEADS, SSD_CHUNK)),
            _resident((1, SSD_INNER)), _resident((1, SSD_INNER))],
        out_specs=tok(SSD_INNER),
        out_shape=jax.ShapeDtypeStruct((t, SSD_INNER), BF16),
        scratch_shapes=[pltpu.VMEM((TS + HIST, SSD_CONV_CH), BF16),
                        pltpu.VMEM((SSD_GROUPS, SSD_STATE, GROUP_W), F32)],
        compiler_params=_params("parallel", "arbitrary"),
        name="ssd",
    )(xbc, z, dt, *consts, conv_w, conv_b, dt_bias_t, a_log_t, d_skip_exp, norm_g)


def _mla_prep_kernel(qc_ref, kvc_ref, kr_ref, pos_ref, inv_ref, qg_ref, kvg_ref, wuq_ref, wuk_ref, wuv_ref,
                     vones_ref, q_out, k_out, v_out):
    ang = pos_ref[...].astype(F32) * inv_ref[...]
    cos = jnp.cos(ang)
    sin = jnp.sin(ang)
    lane = lax.broadcasted_iota(jnp.int32, ang.shape, 1)
    first = (lane >= MLA_NOPE) & (lane < MLA_NOPE + ROPE_HALF)
    second = (lane >= MLA_NOPE + ROPE_HALF) & (lane < MLA_QK)
    q_tab = jnp.where(lane < MLA_NOPE, 1.0, jnp.where(lane < MLA_QK, cos, sin)) * (MLA_QK ** -0.5 * LOG2E)

    kr = kr_ref[...]
    roped = (kr * jnp.where(first | second, cos, 0.0)
             + pltpu.roll(kr, HEAD_PAD - ROPE_HALF, 1) * jnp.where(first, -sin, 0.0)
             + pltpu.roll(kr, ROPE_HALF, 1) * jnp.where(second, sin, 0.0))
    k_rope = roped + pltpu.roll(roped, MLA_ROPE, 1)

    qn = _rms(qc_ref[...].astype(F32), qg_ref[...]).astype(BF16)
    kvn = _rms(kvc_ref[...].astype(F32), kvg_ref[...]).astype(BF16)
    for hp in range(MLA_HEADS // 2):
        cols2 = slice(2 * hp * HEAD_PAD, (2 * hp + 2) * HEAD_PAD)
        q2 = jnp.dot(qn, wuq_ref[:, cols2], preferred_element_type=F32)
        k2 = jnp.dot(kvn, wuk_ref[:, cols2], preferred_element_type=F32)
        v2 = jnp.dot(kvn, wuv_ref[:, cols2], preferred_element_type=F32)
        for hh in range(2):
            cols = slice((2 * hp + hh) * HEAD_PAD, (2 * hp + hh + 1) * HEAD_PAD)
            half = slice(hh * HEAD_PAD, (hh + 1) * HEAD_PAD)
            q_out[:, cols] = (q2[:, half] * q_tab).astype(BF16)
            k_out[:, cols] = (k2[:, half] + k_rope).astype(BF16)
        v_out[:, cols2] = (v2 + vones_ref[:, cols2]).astype(BF16)


def _mla_prep(qc, kvc, kr, pos, inv_lane, q_g, kv_g, wuq_pad, wuk_pad, wuv_pad, v_ones):
    t = qc.shape[0]

    def tok(width):
        return pl.BlockSpec((TM, width), lambda i: (i, 0))

    wide = MLA_HEADS * HEAD_PAD
    return pl.pallas_call(
        _mla_prep_kernel,
        grid=(t // TM,),
        in_specs=[tok(MLA_Q_RANK), tok(MLA_KV_RANK), tok(HEAD_PAD), tok(1), _resident((1, HEAD_PAD)),
                  _resident((1, MLA_Q_RANK)), _resident((1, MLA_KV_RANK)),
                  _resident((MLA_Q_RANK, wide)), _resident((MLA_KV_RANK, wide)), _resident((MLA_KV_RANK, wide)),
                  _resident((1, wide))],
        out_specs=[tok(wide)] * 3,
        out_shape=[jax.ShapeDtypeStruct((t, wide), BF16)] * 3,
        compiler_params=_params("parallel"),
        name="mla_prep",
    )(qc, kvc, kr, pos, inv_lane, q_g, kv_g, wuq_pad, wuk_pad, wuv_pad, v_ones)


def _attn_kernel(q_ref, k_ref, v_ref, o_ref):
    seq = q_ref.shape[0]
    causal = (lax.broadcasted_iota(jnp.int32, (TQ, TQ), 1) <= lax.broadcasted_iota(jnp.int32, (TQ, TQ), 0))
    low_half = lax.broadcasted_iota(jnp.int32, (TQ, HEAD_PAD), 1) < MLA_V
    nt = (((1,), (1,)), ((), ()))
    for i in range(seq // TQ):
        rows = slice(i * TQ, (i + 1) * TQ)
        past = i * TQ
        accs = []
        for hh in range(2):
            hcols = slice(hh * HEAD_PAD, (hh + 1) * HEAD_PAD)
            q = q_ref[rows, hcols]
            s_diag = jnp.where(causal, lax.dot_general(q, k_ref[rows, hcols], nt, preferred_element_type=F32),
                               -jnp.inf)
            m = jnp.max(s_diag, axis=1, keepdims=True)
            if past:
                s_past = lax.dot_general(q, k_ref[0:past, hcols], nt, preferred_element_type=F32)
                m = jnp.maximum(m, jnp.max(s_past, axis=1, keepdims=True))
            acc = jnp.dot(jnp.exp2(s_diag - m).astype(BF16), v_ref[rows, hcols], preferred_element_type=F32)
            if past:
                acc = acc + jnp.dot(jnp.exp2(s_past - m).astype(BF16), v_ref[0:past, hcols],
                                    preferred_element_type=F32)
            accs.append(acc)
        num = jnp.where(low_half, accs[0], accs[1])
        den = pltpu.roll(jnp.where(low_half, accs[1], accs[0]), MLA_V, 1)
        o_ref[rows, :] = (num / den).astype(BF16)


def _attention(qf, kf, vf, batch, seq):
    t = qf.shape[0]
    pair = pl.BlockSpec((seq, 2 * HEAD_PAD), lambda b, hp: (b, hp))
    return pl.pallas_call(
        _attn_kernel,
        grid=(batch, MLA_HEADS // 2),
        in_specs=[pair, pair, pair],
        out_specs=pl.BlockSpec((seq, 2 * MLA_V), lambda b, hp: (b, hp)),
        out_shape=jax.ShapeDtypeStruct((t, MLA_HEADS * MLA_V), BF16),
        compiler_params=_params("parallel", "parallel"),
        name="mla_attention",
    )(qf, kf, vf)


def _merge_kernel(x_ref, yn_ref, o_ref, gate_ref, gb_ref, wssd_ref, wmla_ref, wout_ref, post_g_ref, out_ref):
    y_ssd = jnp.dot(yn_ref[...], wssd_ref[...], preferred_element_type=F32)
    y_mla = jnp.dot(o_ref[...], wmla_ref[...], preferred_element_type=F32)
    gates = jax.nn.sigmoid(gate_ref[...].astype(F32) + gb_ref[...])
    mixed = gates[:, :D_MODEL] * y_ssd + gates[:, D_MODEL:] * y_mla
    h = jnp.dot(mixed.astype(BF16), wout_ref[...], preferred_element_type=F32)
    out_ref[...] = x_ref[...] + _rms(h, post_g_ref[...])


def _merge(x, yn, o, gate, gate_bias, wssd, wmla, wout, post_g):
    t = x.shape[0]

    def tok(width):
        return pl.BlockSpec((TM, width), lambda i: (i, 0))

    sq = _resident((D_MODEL, D_MODEL))
    return pl.pallas_call(
        _merge_kernel,
        grid=(t // TM,),
        in_specs=[tok(D_MODEL), tok(SSD_INNER), tok(MLA_HEADS * MLA_V), tok(2 * D_MODEL),
                  _resident((1, 2 * D_MODEL)), sq, sq, sq, _resident((1, D_MODEL))],
        out_specs=tok(D_MODEL),
        out_shape=jax.ShapeDtypeStruct((t, D_MODEL), F32),
        compiler_params=_params("parallel"),
        name="merge",
    )(x, yn, o, gate, gate_bias, wssd, wmla, wout, post_g)


def _memkv_kernel(mem_ref, g_ref, wk_ref, wv_ref, k_ref, v_ref):
    mn = _rms(mem_ref[...], g_ref[...]).astype(BF16)
    k_ref[...] = jnp.dot(mn, wk_ref[...], preferred_element_type=F32).astype(BF16)
    v_ref[...] = jnp.dot(mn, wv_ref[...], preferred_element_type=F32).astype(BF16)


def _memkv(mem, g, wk, wv, mem_len):
    t = mem.shape[0]
    blk = pl.BlockSpec((mem_len, D_MODEL), lambda b: (b, 0))
    sq = _resident((D_MODEL, D_MODEL))
    return pl.pallas_call(
        _memkv_kernel,
        grid=(t // mem_len,),
        in_specs=[blk, _resident((1, D_MODEL)), sq, sq],
        out_specs=[blk, blk],
        out_shape=[jax.ShapeDtypeStruct((t, D_MODEL), BF16)] * 2,
        compiler_params=_params("parallel"),
        name="memkv",
    )(mem, g, wk, wv)


def _xattn_kernel(x_ref, g_ref, wq_ref, k_ref, v_ref, wo_ref, post_g_ref, out_ref):
    x = x_ref[...]
    hn = _rms(x, g_ref[...]).astype(BF16)
    q = (jnp.dot(hn, wq_ref[...], preferred_element_type=F32) * (XA_HEAD_DIM ** -0.5 * LOG2E)).astype(BF16)
    outs = []
    for h in range(XA_HEADS):
        cols = slice(h * XA_HEAD_DIM, (h + 1) * XA_HEAD_DIM)
        s = lax.dot_general(q[:, cols], k_ref[:, cols], (((1,), (1,)), ((), ())), preferred_element_type=F32)
        p = jnp.exp2(s - jnp.max(s, axis=1, keepdims=True))
        l = jnp.sum(p, axis=1, keepdims=True)
        o = jnp.dot(p.astype(BF16), v_ref[:, cols], preferred_element_type=F32) / l
        outs.append(o.astype(BF16))
    o = jnp.concatenate(outs, axis=1)
    y = jnp.dot(o, wo_ref[...], preferred_element_type=F32)
    out_ref[...] = x + _rms(y, post_g_ref[...])


def _xattn(x, g, wq, k, v, wo, post_g, batch, seq, mem_len):
    t = x.shape[0]
    nblk = seq // TM
    tok = pl.BlockSpec((TM, D_MODEL), lambda b, j: (b * nblk + j, 0))
    kv = pl.BlockSpec((mem_len, D_MODEL), lambda b, j: (b, 0))
    sq = _resident((D_MODEL, D_MODEL))
    return pl.pallas_call(
        _xattn_kernel,
        grid=(batch, nblk),
        in_specs=[tok, _resident((1, D_MODEL)), sq, kv, kv, sq, _resident((1, D_MODEL))],
        out_specs=tok,
        out_shape=jax.ShapeDtypeStruct((t, D_MODEL), F32),
        compiler_params=_params("parallel", "arbitrary"),
        name="xattn",
    )(x, g, wq, k, v, wo, post_g)


def _row(v):
    return v.reshape(1, -1).astype(F32)


def _pad_cols(w, left, right):
    return jnp.pad(w, ((0, 0), (left, right)))


def _layer(x, mem, pos, l, p, batch, seq, mem_len):
    bf = lambda w: w.astype(BF16)

    x = _ffn(x, _row(p["ffn1_pre_g"][l]), bf(p["ffn1_w_gate"][l]), bf(p["ffn1_w_up"][l]),
             bf(p["ffn1_w_down"][l]), _row(p["ffn1_post_g"][l]))

    w_in = p["w_in"][l]
    o_z = 0
    o_xbc = o_z + SSD_INNER
    o_dt = o_xbc + SSD_CONV_CH
    o_qc = o_dt + SSD_HEADS
    o_kvc = o_qc + MLA_Q_RANK
    o_kr = o_kvc + MLA_KV_RANK
    o_gate = o_kr + MLA_ROPE
    w_arr = jnp.concatenate([
        w_in[:, o_z:o_xbc], w_in[:, o_xbc:o_dt], w_in[:, o_gate:], w_in[:, o_qc:o_kvc], w_in[:, o_kvc:o_kr],
        _pad_cols(w_in[:, o_kr:o_gate], MLA_NOPE, HEAD_PAD - MLA_QK),
        _pad_cols(w_in[:, o_dt:o_qc], 0, LANES - SSD_HEADS)], axis=1)
    z, xbc, gate, qc, kvc, kr, dt = _inproj(x, _row(p["mix_pre_g"][l]), bf(w_arr))

    d_skip_exp = jnp.repeat(p["d_skip"][l].astype(F32), SSD_HEAD_DIM).reshape(1, SSD_INNER)
    per_head_rows = lambda v: jnp.broadcast_to(v.astype(F32)[:, None], (SSD_HEADS, SSD_CHUNK))
    yn = _ssd(xbc, z, dt, p["conv_w"][l].astype(F32), _row(p["conv_b"][l]), per_head_rows(p["dt_bias"][l]),
              per_head_rows(p["a_log"][l]), d_skip_exp, _row(p["ssd_norm_g"][l]), batch, seq)

    inv = ROPE_THETA ** (-jnp.arange(0, MLA_ROPE, 2, dtype=F32) / MLA_ROPE)
    inv_lane = jnp.concatenate([jnp.zeros((MLA_NOPE,), F32), inv, inv, inv, inv]).reshape(1, HEAD_PAD)
    wq = p["w_uq"][l].reshape(MLA_Q_RANK, MLA_HEADS, MLA_QK)
    q_x1 = wq[..., MLA_NOPE:MLA_NOPE + ROPE_HALF]
    q_x2 = wq[..., MLA_NOPE + ROPE_HALF:]
    wuq_pad = jnp.concatenate([wq, -q_x2, q_x1], axis=-1).reshape(MLA_Q_RANK, MLA_HEADS * HEAD_PAD)
    wuk_pad = jnp.pad(p["w_uk"][l].reshape(MLA_KV_RANK, MLA_HEADS, MLA_NOPE),
                      ((0, 0), (0, 0), (0, HEAD_PAD - MLA_NOPE))).reshape(MLA_KV_RANK, MLA_HEADS * HEAD_PAD)
    wv = p["w_uv"][l].reshape(MLA_KV_RANK, MLA_HEADS // 2, 2, MLA_V)
    zero_v = jnp.zeros_like(wv[:, :, 0])
    wuv_pad = jnp.stack([jnp.concatenate([wv[:, :, 0], zero_v], axis=-1),
                         jnp.concatenate([zero_v, wv[:, :, 1]], axis=-1)], axis=2)
    wuv_pad = wuv_pad.reshape(MLA_KV_RANK, MLA_HEADS * HEAD_PAD)
    v_ones = jnp.tile(jnp.concatenate([jnp.zeros((MLA_V,), F32), jnp.ones((2 * MLA_V,), F32),
                                       jnp.zeros((MLA_V,), F32)]), MLA_HEADS // 2).reshape(1, MLA_HEADS * HEAD_PAD)
    qf, kf, vf = _mla_prep(qc, kvc, kr, pos, inv_lane, _row(p["q_norm_g"][l]), _row(p["kv_norm_g"][l]),
                           bf(wuq_pad), bf(wuk_pad), bf(wuv_pad), v_ones)
    o = _attention(qf, kf, vf, batch, seq)

    x = _merge(x, yn, o, gate, _row(p["gate_bias"][l]), bf(p["w_ssd_proj"][l]), bf(p["w_mla_proj"][l]),
               bf(p["w_out"][l]), _row(p["mix_post_g"][l]))

    mk, mv = _memkv(mem, _row(p["mem_norm_g"][l]), bf(p["w_xk"][l]), bf(p["w_xv"][l]), mem_len)
    x = _xattn(x, _row(p["xa_pre_g"][l]), bf(p["w_xq"][l]), mk, mv, bf(p["w_xo"][l]), _row(p["xa_post_g"][l]),
               batch, seq, mem_len)

    return _ffn(x, _row(p["ffn2_pre_g"][l]), bf(p["ffn2_w_gate"][l]), bf(p["ffn2_w_up"][l]),
                bf(p["ffn2_w_down"][l]), _row(p["ffn2_post_g"][l]))


def kernel(x, mem, positions, ffn1_pre_g, ffn1_w_gate, ffn1_w_up, ffn1_w_down, ffn1_post_g, mix_pre_g, w_in, conv_w, conv_b, dt_bias, a_log, d_skip, ssd_norm_g, w_ssd_proj, q_norm_g, w_uq, kv_norm_g, w_uk, w_uv, w_mla_proj, gate_bias, w_out, mix_post_g, xa_pre_g, mem_norm_g, w_xq, w_xk, w_xv, w_xo, xa_post_g, ffn2_pre_g, ffn2_w_gate, ffn2_w_up, ffn2_w_down, ffn2_post_g):
    p = dict(ffn1_pre_g=ffn1_pre_g, ffn1_w_gate=ffn1_w_gate, ffn1_w_up=ffn1_w_up, ffn1_w_down=ffn1_w_down,
             ffn1_post_g=ffn1_post_g, mix_pre_g=mix_pre_g, w_in=w_in, conv_w=conv_w, conv_b=conv_b,
             dt_bias=dt_bias, a_log=a_log, d_skip=d_skip, ssd_norm_g=ssd_norm_g, w_ssd_proj=w_ssd_proj,
             q_norm_g=q_norm_g, w_uq=w_uq, kv_norm_g=kv_norm_g, w_uk=w_uk, w_uv=w_uv, w_mla_proj=w_mla_proj,
             gate_bias=gate_bias, w_out=w_out, mix_post_g=mix_post_g, xa_pre_g=xa_pre_g, mem_norm_g=mem_norm_g,
             w_xq=w_xq, w_xk=w_xk, w_xv=w_xv, w_xo=w_xo, xa_post_g=xa_post_g, ffn2_pre_g=ffn2_pre_g,
             ffn2_w_gate=ffn2_w_gate, ffn2_w_up=ffn2_w_up, ffn2_w_down=ffn2_w_down, ffn2_post_g=ffn2_post_g)
    batch, seq, _ = x.shape
    mem_len = mem.shape[1]
    assert seq % TS == 0 and seq % TM == 0 and seq % TQ == 0 and TS % SSD_CHUNK == 0
    xf = x.reshape(batch * seq, D_MODEL)
    memf = mem.reshape(batch * mem_len, D_MODEL)
    pos = positions.reshape(batch * seq, 1)
    for l in range(w_in.shape[0]):
        xf = _layer(xf, memf, pos, l, p, batch, seq, mem_len)
    return xf.reshape(batch, seq, D_MODEL)
```

```python
import math

import numpy as np
import jax
import jax.numpy as jnp
from jax import lax
from jax.experimental import pallas as pl
from jax.experimental.pallas import tpu as pltpu

F32 = jnp.float32
BF16 = jnp.bfloat16

D_MODEL = 1024
SSD_HEADS = 16
SSD_HEAD_DIM = 64
SSD_INNER = SSD_HEADS * SSD_HEAD_DIM
SSD_GROUPS = 2
SSD_STATE = 128
SSD_CONV = 4
SSD_CHUNK = 128
SSD_CONV_CH = SSD_INNER + 2 * SSD_GROUPS * SSD_STATE
MLA_HEADS = 16
MLA_Q_RANK = 384
MLA_KV_RANK = 256
MLA_NOPE = 64
MLA_ROPE = 32
MLA_V = 64
MLA_QK = MLA_NOPE + MLA_ROPE
ROPE_THETA = 10000.0
XA_HEADS = 4
XA_HEAD_DIM = D_MODEL // XA_HEADS
D_FF = 2816
FFN_RES_WEIGHT = 0.5
EPS = 1e-6

LANES = 128
BF16_ROWS = 16
HEAD_PAD = LANES
ROPE_HALF = MLA_ROPE // 2
GROUP_W = SSD_INNER // SSD_GROUPS
PAIRS_PER_GROUP = GROUP_W // LANES
LOG2E = math.log2(math.e)

TM = 1024
FFN_CHUNK = 256
TS = 512
TQ = 256
ATTN_LOOKAHEAD = 3
HIST = BF16_ROWS
VMEM_LIMIT = 56 * 1024 * 1024

_C_Z = (0, SSD_INNER)
_C_XBC = (_C_Z[1], _C_Z[1] + SSD_CONV_CH)
_C_GATE = (_C_XBC[1], _C_XBC[1] + 2 * D_MODEL)
_C_QC = (_C_GATE[1], _C_GATE[1] + MLA_Q_RANK)
_C_KVC = (_C_QC[1], _C_QC[1] + MLA_KV_RANK)
_C_KR = (_C_KVC[1], _C_KVC[1] + HEAD_PAD)
_C_DT = (_C_KR[1], _C_KR[1] + LANES)
D_IN_ARR = _C_DT[1]

_ROWS_DT, _ROWS_ECS, _ROWS_DSF = (0, 32), (32, 64), (64, 96)
_ROWS_CS = (0, 48)


def _rms(x, g):
    return x * lax.rsqrt(jnp.mean(x * x, axis=-1, keepdims=True) + EPS) * g


def _softplus(x):
    return jnp.maximum(x, 0.0) + jnp.log1p(jnp.exp(-jnp.abs(x)))


def _bf16_pieces(x, n):
    pieces, rest = [], x
    for _ in range(n):
        piece = rest.astype(BF16).astype(F32)
        pieces.append(piece)
        rest = rest - piece
    return pieces


def _resident(shape):
    zeros = (0,) * len(shape)
    return pl.BlockSpec(shape, lambda *_: zeros, pipeline_mode=pl.Buffered(1))


def _params(*sem):
    return pltpu.CompilerParams(dimension_semantics=sem, vmem_limit_bytes=VMEM_LIMIT)


def _ffn_kernel(x_ref, pre_g_ref, wg_ref, wu_ref, wd_ref, post_g_ref, o_ref, h_ref):
    half = TM // 2
    parts = [slice(0, half), slice(half, TM)]
    xs = [x_ref[r, :] for r in parts]
    xns = [_rms(x, pre_g_ref[...]).astype(BF16) for x in xs]
    for c in range(D_FF // FFN_CHUNK):
        sl = slice(c * FFN_CHUNK, (c + 1) * FFN_CHUNK)
        for r, xn in zip(parts, xns):
            g = jnp.dot(xn, wg_ref[:, sl], preferred_element_type=F32)
            u = jnp.dot(xn, wu_ref[:, sl], preferred_element_type=F32)
            h_ref[r, sl] = (jax.nn.silu(g) * u).astype(BF16)
    ys = [jnp.dot(h_ref[r, :], wd_ref[...], preferred_element_type=F32) for r in parts]
    for r, x, y in zip(parts, xs, ys):
        o_ref[r, :] = x + FFN_RES_WEIGHT * _rms(y, post_g_ref[...])


def _ffn(x, pre_g, wg, wu, wd, post_g):
    t = x.shape[0]
    tok = pl.BlockSpec((TM, D_MODEL), lambda i: (i, 0))
    return pl.pallas_call(
        _ffn_kernel,
        grid=(t // TM,),
        in_specs=[tok, _resident((1, D_MODEL)), _resident((D_MODEL, D_FF)), _resident((D_MODEL, D_FF)),
                  _resident((D_FF, D_MODEL)), _resident((1, D_MODEL))],
        out_specs=tok,
        out_shape=jax.ShapeDtypeStruct((t, D_MODEL), F32),
        scratch_shapes=[pltpu.VMEM((TM, D_FF), BF16)],
        compiler_params=_params("parallel"),
        name="ffn",
    )(x, pre_g, wg, wu, wd, post_g)


def _inproj_kernel(x_ref, g_ref, w_ref, z_ref, xbc_ref, gate_ref, qc_ref, kvc_ref, kr_ref, dt_ref):
    hn = _rms(x_ref[...], g_ref[...]).astype(BF16)

    def proj(cols):
        return jnp.dot(hn, w_ref[:, cols[0]:cols[1]], preferred_element_type=F32)

    z_ref[...] = proj(_C_Z).astype(BF16)
    xbc_ref[...] = proj(_C_XBC).astype(BF16)
    gate_ref[...] = proj(_C_GATE).astype(BF16)
    qc_ref[...] = proj(_C_QC).astype(BF16)
    kvc_ref[...] = proj(_C_KVC).astype(BF16)
    kr_ref[...] = proj(_C_KR)
    dt_ref[...] = proj(_C_DT)


def _inproj(x, g, w_arr):
    t = x.shape[0]

    def tok(width):
        return pl.BlockSpec((TM, width), lambda i: (i, 0))

    widths = (SSD_INNER, SSD_CONV_CH, 2 * D_MODEL, MLA_Q_RANK, MLA_KV_RANK, HEAD_PAD, LANES)
    dtypes = (BF16, BF16, BF16, BF16, BF16, F32, F32)
    return pl.pallas_call(
        _inproj_kernel,
        grid=(t // TM,),
        in_specs=[tok(D_MODEL), _resident((1, D_MODEL)), _resident((D_MODEL, D_IN_ARR))],
        out_specs=[tok(w) for w in widths],
        out_shape=[jax.ShapeDtypeStruct((t, w), d) for w, d in zip(widths, dtypes)],
        compiler_params=_params("parallel"),
        name="inproj",
    )(x, g, w_arr)


def _ssd_constants():
    L = SSD_CHUNK
    shift = np.zeros(((SSD_CONV - 1) * L, L + HIST), np.float32)
    for k in range(SSD_CONV - 1):
        shift[k * L + np.arange(L), np.arange(L) + HIST - (SSD_CONV - 1) + k] = 1.0
    head_of_lane = np.arange(SSD_INNER) // SSD_HEAD_DIM
    exp_ch = np.zeros((LANES, 3 * SSD_INNER), np.float32)
    for q, (lo, hi) in enumerate((_ROWS_DT, _ROWS_ECS, _ROWS_DSF)):
        for r in range(lo, hi):
            exp_ch[r, q * SSD_INNER + np.nonzero(head_of_lane == r % SSD_HEADS)[0]] = 1.0
    exp_blk = np.zeros((LANES, SSD_HEADS * LANES), np.float32)
    for r in range(*_ROWS_CS):
        h = r % SSD_HEADS
        exp_blk[r, h * LANES:(h + 1) * LANES] = 1.0
    tri = np.tile((np.arange(L)[:, None] <= np.arange(L)[None, :]).astype(np.float32), (3, 1))
    return tuple(jnp.asarray(m, BF16) for m in (shift, exp_ch, exp_blk, tri))


def _ssd_kernel(xbc_ref, z_ref, dt_ref, shift_ref, expch_ref, expblk_ref, tri_ref, cw_ref, cb_ref, dtb_ref,
                alog_ref, dskip_ref, ng_ref, y_ref, xpad, state):
    L = SSD_CHUNK

    @pl.when(pl.program_id(1) == 0)
    def _start_of_sequence():
        xpad[0:HIST, :] = jnp.zeros((HIST, SSD_CONV_CH), BF16)
        state[...] = jnp.zeros_like(state)

    xpad[HIST:HIST + TS, :] = xbc_ref[...]

    low_half = lax.broadcasted_iota(jnp.int32, (L, LANES), 1) < SSD_HEAD_DIM
    tril = lax.broadcasted_iota(jnp.int32, (L, L), 1) <= lax.broadcasted_iota(jnp.int32, (L, L), 0)
    a_neg = -jnp.exp(alog_ref[...])

    def to_token_rows(stack):
        full = jnp.concatenate([stack, jnp.zeros((LANES - stack.shape[0], L), F32)], axis=0)
        return full.T.astype(BF16)

    def chunk(c, carry):
        r0 = pl.multiple_of(c * L, L)
        rows = pl.ds(r0, L)
        win = xpad[pl.ds(r0, L + HIST), :]
        taps = jnp.dot(shift_ref[...], win, preferred_element_type=F32)
        u = cb_ref[...] + cw_ref[SSD_CONV - 1:SSD_CONV, :] * win[HIST:, :].astype(F32)
        for k in range(SSD_CONV - 1):
            u = u + cw_ref[k:k + 1, :] * taps[k * L:(k + 1) * L, :]
        u = jax.nn.silu(u)
        xs = u[:, :SSD_INNER]
        bmat = u[:, SSD_INNER:SSD_INNER + SSD_GROUPS * SSD_STATE]
        cmat = u[:, SSD_INNER + SSD_GROUPS * SSD_STATE:]

        dt_t = _softplus(dt_ref[rows, :].T[:SSD_HEADS, :] + dtb_ref[...])
        adt_t = dt_t * a_neg
        cs_t = jnp.dot(jnp.concatenate(_bf16_pieces(adt_t, 3), axis=1).astype(BF16), tri_ref[...],
                       preferred_element_type=F32)
        ecs_t = jnp.exp(cs_t)
        dsf_t = jnp.exp(cs_t[:, L - 1:L] - cs_t)
        stack = jnp.concatenate(_bf16_pieces(dt_t, 2) + _bf16_pieces(ecs_t, 2) + _bf16_pieces(dsf_t, 2), axis=0)
        per_ch = jnp.dot(to_token_rows(stack), expch_ref[...], preferred_element_type=F32)
        dt_e = per_ch[:, :SSD_INNER]
        ecs_e = per_ch[:, SSD_INNER:2 * SSD_INNER]
        dsf_e = per_ch[:, 2 * SSD_INNER:]
        cs_col = jnp.dot(to_token_rows(jnp.concatenate(_bf16_pieces(cs_t, 3), axis=0)), expblk_ref[...],
                         preferred_element_type=F32)
        z = z_ref[rows, :].astype(F32)

        for g in range(SSD_GROUPS):
            gcols = slice(g * GROUP_W, (g + 1) * GROUP_W)
            bm_g = bmat[:, g * SSD_STATE:(g + 1) * SSD_STATE]
            cm_g = cmat[:, g * SSD_STATE:(g + 1) * SSD_STATE].astype(BF16)
            cbm = lax.dot_general(cm_g, bm_g.astype(BF16), (((1,), (1,)), ((), ())),
                                  preferred_element_type=F32)
            s_g = state[g]
            y_off = jnp.dot(cm_g, s_g.astype(BF16), preferred_element_type=F32)
            xs_g = xs[:, gcols]
            xdt_g = xs_g * dt_e[:, gcols]
            xw_g = (xdt_g * dsf_e[:, gcols]).astype(BF16)
            state[g] = (s_g * ecs_e[L - 1:L, gcols]
                        + jnp.dot(bm_g.T.astype(BF16), xw_g, preferred_element_type=F32))
            y_parts = []
            for pp in range(PAIRS_PER_GROUP):
                p = g * PAIRS_PER_GROUP + pp

                def masked(h):
                    dec = jnp.exp(jnp.where(tril, cs_col[:, h * LANES:(h + 1) * LANES] - cs_t[h:h + 1, :],
                                            -jnp.inf))
                    return (cbm * dec).astype(BF16)

                xdt_p = xdt_g[:, pp * LANES:(pp + 1) * LANES]
                m2 = jnp.concatenate([masked(2 * p), masked(2 * p + 1)], axis=1)
                x_blk = jnp.concatenate([jnp.where(low_half, xdt_p, 0.0), jnp.where(low_half, 0.0, xdt_p)],
                                        axis=0).astype(BF16)
                y_parts.append(jnp.dot(m2, x_blk, preferred_element_type=F32))
            y_g = (jnp.concatenate(y_parts, axis=1) + y_off * ecs_e[:, gcols] + dskip_ref[:, gcols] * xs_g)
            y_g = y_g * jax.nn.silu(z[:, gcols])
            y_ref[rows, gcols] = _rms(y_g, ng_ref[:, gcols]).astype(BF16)
        return carry

    lax.fori_loop(0, TS // L, chunk, 0)
    xpad[0:HIST, :] = xpad[TS:TS + HIST, :]


def _ssd(xbc, z, dt, conv_w, conv_b, dt_bias_t, a_log_t, d_skip_exp, norm_g, batch, seq):
    t = xbc.shape[0]
    nblk = seq // TS
    consts = _ssd_constants()

    def tok(width):
        return pl.BlockSpec((TS, width), lambda b, j: (b * nblk + j, 0))

    return pl.pallas_call(
        _ssd_kernel,
        grid=(batch, nblk),
        in_specs=[tok(SSD_CONV_CH), tok(SSD_INNER), tok(LANES)] + [_resident(m.shape) for m in consts] + [
            _resident((SSD_CONV, SSD_CONV_CH)), _resident((1, SSD_CONV_CH)),
            _resident((SSD_HEADS, SSD_CHUNK)), _resident((SSD_HEADS, SSD_CHUNK)),
            _resident((1, SSD_INNER)), _resident((1, SSD_INNER))],
        out_specs=tok(SSD_INNER),
        out_shape=jax.ShapeDtypeStruct((t, SSD_INNER), BF16),
        scratch_shapes=[pltpu.VMEM((TS + HIST, SSD_CONV_CH), BF16),
                        pltpu.VMEM((SSD_GROUPS, SSD_STATE, GROUP_W), F32)],
        compiler_params=_params("parallel", "arbitrary"),
        name="ssd",
    )(xbc, z, dt, *consts, conv_w, conv_b, dt_bias_t, a_log_t, d_skip_exp, norm_g)


def _rope_constants():
    e = np.zeros((LANES, 2 * HEAD_PAD), np.float32)
    for piece in range(3):
        for j in range(ROPE_HALF):
            c_row = piece * ROPE_HALF + j
            s_row = 3 * ROPE_HALF + piece * ROPE_HALF + j
            e[c_row, MLA_NOPE + j] = e[c_row, MLA_NOPE + ROPE_HALF + j] = 1.0
            e[s_row, MLA_QK + j] = e[s_row, MLA_QK + ROPE_HALF + j] = 1.0
            e[s_row, HEAD_PAD + MLA_NOPE + j] = -1.0
            e[s_row, HEAD_PAD + MLA_NOPE + ROPE_HALF + j] = 1.0
    return jnp.asarray(e, BF16)


def _mla_prep_kernel(qc_ref, kvc_ref, kr_ref, pos_ref, inv_ref, erope_ref, qg_ref, kvg_ref, wuq_ref, wuk_ref,
                     wuv_ref, vones_ref, q_out, k_out, v_out):
    ang_t = inv_ref[...] * pos_ref[0].astype(F32)
    stack = jnp.concatenate(_bf16_pieces(jnp.cos(ang_t), 3) + _bf16_pieces(jnp.sin(ang_t), 3)
                            + [jnp.zeros((LANES - 6 * ROPE_HALF, TM), F32)], axis=0)
    tabs = jnp.dot(stack.T.astype(BF16), erope_ref[...], preferred_element_type=F32)
    q_raw = tabs[:, :HEAD_PAD]
    k_sin = tabs[:, HEAD_PAD:]
    lane = lax.broadcasted_iota(jnp.int32, q_raw.shape, 1)
    first = (lane >= MLA_NOPE) & (lane < MLA_NOPE + ROPE_HALF)
    q_tab = jnp.where(lane < MLA_NOPE, 1.0, q_raw) * (MLA_QK ** -0.5 * LOG2E)

    kr = kr_ref[...]
    swapped = jnp.where(first, pltpu.roll(kr, HEAD_PAD - ROPE_HALF, 1), pltpu.roll(kr, ROPE_HALF, 1))
    roped = kr * q_raw + swapped * k_sin
    k_rope = roped + pltpu.roll(roped, MLA_ROPE, 1)

    qn = _rms(qc_ref[...].astype(F32), qg_ref[...]).astype(BF16)
    kvn = _rms(kvc_ref[...].astype(F32), kvg_ref[...]).astype(BF16)
    for hp in range(MLA_HEADS // 2):
        cols2 = slice(2 * hp * HEAD_PAD, (2 * hp + 2) * HEAD_PAD)
        q2 = jnp.dot(qn, wuq_ref[:, cols2], preferred_element_type=F32)
        k2 = jnp.dot(kvn, wuk_ref[:, cols2], preferred_element_type=F32)
        v2 = jnp.dot(kvn, wuv_ref[:, cols2], preferred_element_type=F32)
        for hh in range(2):
            cols = slice((2 * hp + hh) * HEAD_PAD, (2 * hp + hh + 1) * HEAD_PAD)
            half = slice(hh * HEAD_PAD, (hh + 1) * HEAD_PAD)
            q_out[:, cols] = (q2[:, half] * q_tab).astype(BF16)
            k_out[:, cols] = (k2[:, half] + k_rope).astype(BF16)
        v_out[:, cols2] = (v2 + vones_ref[:, cols2]).astype(BF16)


def _mla_prep(qc, kvc, kr, pos_rows, inv_col, q_g, kv_g, wuq_pad, wuk_pad, wuv_pad, v_ones):
    t = qc.shape[0]
    e_rope = _rope_constants()

    def tok(width):
        return pl.BlockSpec((TM, width), lambda i: (i, 0))

    wide = MLA_HEADS * HEAD_PAD
    return pl.pallas_call(
        _mla_prep_kernel,
        grid=(t // TM,),
        in_specs=[tok(MLA_Q_RANK), tok(MLA_KV_RANK), tok(HEAD_PAD), pl.BlockSpec((1, 1, TM), lambda i: (i, 0, 0)),
                  _resident((ROPE_HALF, 1)), _resident(e_rope.shape),
                  _resident((1, MLA_Q_RANK)), _resident((1, MLA_KV_RANK)),
                  _resident((MLA_Q_RANK, wide)), _resident((MLA_KV_RANK, wide)), _resident((MLA_KV_RANK, wide)),
                  _resident((1, wide))],
        out_specs=[tok(wide)] * 3,
        out_shape=[jax.ShapeDtypeStruct((t, wide), BF16)] * 3,
        compiler_params=_params("parallel"),
        name="mla_prep",
    )(qc, kvc, kr, pos_rows, inv_col, e_rope, q_g, kv_g, wuq_pad, wuk_pad, wuv_pad, v_ones)


def _attn_kernel(q_ref, k_ref, v_ref, o_ref):
    seq = q_ref.shape[0]
    causal = (lax.broadcasted_iota(jnp.int32, (TQ, TQ), 1) <= lax.broadcasted_iota(jnp.int32, (TQ, TQ), 0))
    low_half = lax.broadcasted_iota(jnp.int32, (TQ, HEAD_PAD), 1) < MLA_V
    nt = (((1,), (1,)), ((), ()))
    units = [(i, hh) for i in reversed(range(seq // TQ)) for hh in range(2)]

    def scores(i, hh):
        rows = slice(i * TQ, (i + 1) * TQ)
        hcols = slice(hh * HEAD_PAD, (hh + 1) * HEAD_PAD)
        q = q_ref[rows, hcols]
        s_diag = jnp.where(causal, lax.dot_general(q, k_ref[rows, hcols], nt, preferred_element_type=F32),
                           -jnp.inf)
        s_past = lax.dot_general(q, k_ref[0:i * TQ, hcols], nt, preferred_element_type=F32) if i else None
        return s_diag, s_past

    def weighted_values(i, hh, s_diag, s_past):
        rows = slice(i * TQ, (i + 1) * TQ)
        hcols = slice(hh * HEAD_PAD, (hh + 1) * HEAD_PAD)
        m = jnp.max(s_diag, axis=1, keepdims=True)
        if i:
            m = jnp.maximum(m, jnp.max(s_past, axis=1, keepdims=True))
        acc = jnp.dot(jnp.exp2(s_diag - m).astype(BF16), v_ref[rows, hcols], preferred_element_type=F32)
        if i:
            acc = acc + jnp.dot(jnp.exp2(s_past - m).astype(BF16), v_ref[0:i * TQ, hcols],
                                preferred_element_type=F32)
        return acc

    pending = [scores(*unit) for unit in units[:ATTN_LOOKAHEAD]]
    accs = []
    for n, (i, hh) in enumerate(units):
        current = pending.pop(0)
        if n + ATTN_LOOKAHEAD < len(units):
            pending.append(scores(*units[n + ATTN_LOOKAHEAD]))
        accs.append(weighted_values(i, hh, *current))
        if hh == 1:
            num = jnp.where(low_half, accs[0], accs[1])
            den = pltpu.roll(jnp.where(low_half, accs[1], accs[0]), MLA_V, 1)
            o_ref[i * TQ:(i + 1) * TQ, :] = (num / den).astype(BF16)
            accs = []


def _attention(qf, kf, vf, batch, seq):
    t = qf.shape[0]
    pair = pl.BlockSpec((seq, 2 * HEAD_PAD), lambda b, hp: (b, hp))
    return pl.pallas_call(
        _attn_kernel,
        grid=(batch, MLA_HEADS // 2),
        in_specs=[pair, pair, pair],
        out_specs=pl.BlockSpec((seq, 2 * MLA_V), lambda b, hp: (b, hp)),
        out_shape=jax.ShapeDtypeStruct((t, MLA_HEADS * MLA_V), BF16),
        compiler_params=_params("parallel", "parallel"),
        name="mla_attention",
    )(qf, kf, vf)


def _merge_kernel(x_ref, yn_ref, o_ref, gate_ref, gb_ref, wssd_ref, wmla_ref, wout_ref, post_g_ref, out_ref):
    y_ssd = jnp.dot(yn_ref[...], wssd_ref[...], preferred_element_type=F32)
    y_mla = jnp.dot(o_ref[...], wmla_ref[...], preferred_element_type=F32)
    gates = jax.nn.sigmoid(gate_ref[...].astype(F32) + gb_ref[...])
    mixed = gates[:, :D_MODEL] * y_ssd + gates[:, D_MODEL:] * y_mla
    h = jnp.dot(mixed.astype(BF16), wout_ref[...], preferred_element_type=F32)
    out_ref[...] = x_ref[...] + _rms(h, post_g_ref[...])


def _merge(x, yn, o, gate, gate_bias, wssd, wmla, wout, post_g):
    t = x.shape[0]

    def tok(width):
        return pl.BlockSpec((TM, width), lambda i: (i, 0))

    sq = _resident((D_MODEL, D_MODEL))
    return pl.pallas_call(
        _merge_kernel,
        grid=(t // TM,),
        in_specs=[tok(D_MODEL), tok(SSD_INNER), tok(MLA_HEADS * MLA_V), tok(2 * D_MODEL),
                  _resident((1, 2 * D_MODEL)), sq, sq, sq, _resident((1, D_MODEL))],
        out_specs=tok(D_MODEL),
        out_shape=jax.ShapeDtypeStruct((t, D_MODEL), F32),
        compiler_params=_params("parallel"),
        name="merge",
    )(x, yn, o, gate, gate_bias, wssd, wmla, wout, post_g)


def _memkv_kernel(mem_ref, g_ref, wk_ref, wv_ref, k_ref, v_ref):
    mn = _rms(mem_ref[...], g_ref[...]).astype(BF16)
    k_ref[...] = jnp.dot(mn, wk_ref[...], preferred_element_type=F32).astype(BF16)
    v_ref[...] = jnp.dot(mn, wv_ref[...], preferred_element_type=F32).astype(BF16)


def _memkv(mem, g, wk, wv, mem_len):
    t = mem.shape[0]
    blk = pl.BlockSpec((mem_len, D_MODEL), lambda b: (b, 0))
    sq = _resident((D_MODEL, D_MODEL))
    return pl.pallas_call(
        _memkv_kernel,
        grid=(t // mem_len,),
        in_specs=[blk, _resident((1, D_MODEL)), sq, sq],
        out_specs=[blk, blk],
        out_shape=[jax.ShapeDtypeStruct((t, D_MODEL), BF16)] * 2,
        compiler_params=_params("parallel"),
        name="memkv",
    )(mem, g, wk, wv)


def _xattn_kernel(x_ref, g_ref, wq_ref, k_ref, v_ref, wo_ref, post_g_ref, out_ref):
    x = x_ref[...]
    hn = _rms(x, g_ref[...]).astype(BF16)
    q = (jnp.dot(hn, wq_ref[...], preferred_element_type=F32) * (XA_HEAD_DIM ** -0.5 * LOG2E)).astype(BF16)
    outs = []
    heads = [slice(h * XA_HEAD_DIM, (h + 1) * XA_HEAD_DIM) for h in range(XA_HEADS)]
    scores = [lax.dot_general(q[:, cols], k_ref[:, cols], (((1,), (1,)), ((), ())), preferred_element_type=F32)
              for cols in heads]
    for cols, s in zip(heads, scores):
        p = jnp.exp2(s - jnp.max(s, axis=1, keepdims=True))
        l = jnp.sum(p, axis=1, keepdims=True)
        o = jnp.dot(p.astype(BF16), v_ref[:, cols], preferred_element_type=F32) / l
        outs.append(o.astype(BF16))
    o = jnp.concatenate(outs, axis=1)
    y = jnp.dot(o, wo_ref[...], preferred_element_type=F32)
    out_ref[...] = x + _rms(y, post_g_ref[...])


def _xattn(x, g, wq, k, v, wo, post_g, batch, seq, mem_len):
    t = x.shape[0]
    nblk = seq // TM
    tok = pl.BlockSpec((TM, D_MODEL), lambda b, j: (b * nblk + j, 0))
    kv = pl.BlockSpec((mem_len, D_MODEL), lambda b, j: (b, 0))
    sq = _resident((D_MODEL, D_MODEL))
    return pl.pallas_call(
        _xattn_kernel,
        grid=(batch, nblk),
        in_specs=[tok, _resident((1, D_MODEL)), sq, kv, kv, sq, _resident((1, D_MODEL))],
        out_specs=tok,
        out_shape=jax.ShapeDtypeStruct((t, D_MODEL), F32),
        compiler_params=_params("parallel", "arbitrary"),
        name="xattn",
    )(x, g, wq, k, v, wo, post_g)


def _row(v):
    return v.reshape(1, -1).astype(F32)


def _pad_cols(w, left, right):
    return jnp.pad(w, ((0, 0), (left, right)))


def _layer(x, mem, pos, l, p, batch, seq, mem_len):
    bf = lambda w: w.astype(BF16)

    x = _ffn(x, _row(p["ffn1_pre_g"][l]), bf(p["ffn1_w_gate"][l]), bf(p["ffn1_w_up"][l]),
             bf(p["ffn1_w_down"][l]), _row(p["ffn1_post_g"][l]))

    w_in = p["w_in"][l]
    o_z = 0
    o_xbc = o_z + SSD_INNER
    o_dt = o_xbc + SSD_CONV_CH
    o_qc = o_dt + SSD_HEADS
    o_kvc = o_qc + MLA_Q_RANK
    o_kr = o_kvc + MLA_KV_RANK
    o_gate = o_kr + MLA_ROPE
    w_arr = jnp.concatenate([
        w_in[:, o_z:o_xbc], w_in[:, o_xbc:o_dt], w_in[:, o_gate:], w_in[:, o_qc:o_kvc], w_in[:, o_kvc:o_kr],
        _pad_cols(w_in[:, o_kr:o_gate], MLA_NOPE, HEAD_PAD - MLA_QK),
        _pad_cols(w_in[:, o_dt:o_qc], 0, LANES - SSD_HEADS)], axis=1)
    z, xbc, gate, qc, kvc, kr, dt = _inproj(x, _row(p["mix_pre_g"][l]), bf(w_arr))

    d_skip_exp = jnp.repeat(p["d_skip"][l].astype(F32), SSD_HEAD_DIM).reshape(1, SSD_INNER)
    per_head_rows = lambda v: jnp.broadcast_to(v.astype(F32)[:, None], (SSD_HEADS, SSD_CHUNK))
    yn = _ssd(xbc, z, dt, p["conv_w"][l].astype(F32), _row(p["conv_b"][l]), per_head_rows(p["dt_bias"][l]),
              per_head_rows(p["a_log"][l]), d_skip_exp, _row(p["ssd_norm_g"][l]), batch, seq)

    inv = ROPE_THETA ** (-jnp.arange(0, MLA_ROPE, 2, dtype=F32) / MLA_ROPE)
    inv_col = inv.reshape(ROPE_HALF, 1)
    wq = p["w_uq"][l].reshape(MLA_Q_RANK, MLA_HEADS, MLA_QK)
    q_x1 = wq[..., MLA_NOPE:MLA_NOPE + ROPE_HALF]
    q_x2 = wq[..., MLA_NOPE + ROPE_HALF:]
    wuq_pad = jnp.concatenate([wq, -q_x2, q_x1], axis=-1).reshape(MLA_Q_RANK, MLA_HEADS * HEAD_PAD)
    wuk_pad = jnp.pad(p["w_uk"][l].reshape(MLA_KV_RANK, MLA_HEADS, MLA_NOPE),
                      ((0, 0), (0, 0), (0, HEAD_PAD - MLA_NOPE))).reshape(MLA_KV_RANK, MLA_HEADS * HEAD_PAD)
    wv = p["w_uv"][l].reshape(MLA_KV_RANK, MLA_HEADS // 2, 2, MLA_V)
    zero_v = jnp.zeros_like(wv[:, :, 0])
    wuv_pad = jnp.stack([jnp.concatenate([wv[:, :, 0], zero_v], axis=-1),
                         jnp.concatenate([zero_v, wv[:, :, 1]], axis=-1)], axis=2)
    wuv_pad = wuv_pad.reshape(MLA_KV_RANK, MLA_HEADS * HEAD_PAD)
    v_ones = jnp.tile(jnp.concatenate([jnp.zeros((MLA_V,), F32), jnp.ones((2 * MLA_V,), F32),
                                       jnp.zeros((MLA_V,), F32)]), MLA_HEADS // 2).reshape(1, MLA_HEADS * HEAD_PAD)
    qf, kf, vf = _mla_prep(qc, kvc, kr, pos, inv_col,_row(p["q_norm_g"][l]), _row(p["kv_norm_g"][l]),
                           bf(wuq_pad), bf(wuk_pad), bf(wuv_pad), v_ones)
    o = _attention(qf, kf, vf, batch, seq)

    x = _merge(x, yn, o, gate, _row(p["gate_bias"][l]), bf(p["w_ssd_proj"][l]), bf(p["w_mla_proj"][l]),
               bf(p["w_out"][l]), _row(p["mix_post_g"][l]))

    mk, mv = _memkv(mem, _row(p["mem_norm_g"][l]), bf(p["w_xk"][l]), bf(p["w_xv"][l]), mem_len)
    x = _xattn(x, _row(p["xa_pre_g"][l]), bf(p["w_xq"][l]), mk, mv, bf(p["w_xo"][l]), _row(p["xa_post_g"][l]),
               batch, seq, mem_len)

    return _ffn(x, _row(p["ffn2_pre_g"][l]), bf(p["ffn2_w_gate"][l]), bf(p["ffn2_w_up"][l]),
                bf(p["ffn2_w_down"][l]), _row(p["ffn2_post_g"][l]))


def kernel(x, mem, positions, ffn1_pre_g, ffn1_w_gate, ffn1_w_up, ffn1_w_down, ffn1_post_g, mix_pre_g, w_in, conv_w, conv_b, dt_bias, a_log, d_skip, ssd_norm_g, w_ssd_proj, q_norm_g, w_uq, kv_norm_g, w_uk, w_uv, w_mla_proj, gate_bias, w_out, mix_post_g, xa_pre_g, mem_norm_g, w_xq, w_xk, w_xv, w_xo, xa_post_g, ffn2_pre_g, ffn2_w_gate, ffn2_w_up, ffn2_w_down, ffn2_post_g):
    p = dict(ffn1_pre_g=ffn1_pre_g, ffn1_w_gate=ffn1_w_gate, ffn1_w_up=ffn1_w_up, ffn1_w_down=ffn1_w_down,
             ffn1_post_g=ffn1_post_g, mix_pre_g=mix_pre_g, w_in=w_in, conv_w=conv_w, conv_b=conv_b,
             dt_bias=dt_bias, a_log=a_log, d_skip=d_skip, ssd_norm_g=ssd_norm_g, w_ssd_proj=w_ssd_proj,
             q_norm_g=q_norm_g, w_uq=w_uq, kv_norm_g=kv_norm_g, w_uk=w_uk, w_uv=w_uv, w_mla_proj=w_mla_proj,
             gate_bias=gate_bias, w_out=w_out, mix_post_g=mix_post_g, xa_pre_g=xa_pre_g, mem_norm_g=mem_norm_g,
             w_xq=w_xq, w_xk=w_xk, w_xv=w_xv, w_xo=w_xo, xa_post_g=xa_post_g, ffn2_pre_g=ffn2_pre_g,
             ffn2_w_gate=ffn2_w_gate, ffn2_w_up=ffn2_w_up, ffn2_w_down=ffn2_w_down, ffn2_post_g=ffn2_post_g)
    batch, seq, _ = x.shape
    mem_len = mem.shape[1]
    assert seq % TS == 0 and seq % TM == 0 and seq % TQ == 0 and TS % SSD_CHUNK == 0
    xf = x.reshape(batch * seq, D_MODEL)
    memf = mem.reshape(batch * mem_len, D_MODEL)
    pos = positions.reshape(batch * seq // TM, 1, TM)
    for l in range(w_in.shape[0]):
        xf = _layer(xf, memf, pos, l, p, batch, seq, mem_len)
    return xf.reshape(batch, seq, D_MODEL)
```

```python
import math

import numpy as np
import jax
import jax.numpy as jnp
from jax import lax
from jax.experimental import pallas as pl
from jax.experimental.pallas import tpu as pltpu

F32 = jnp.float32
BF16 = jnp.bfloat16

D_MODEL = 1024
SSD_HEADS = 16
SSD_HEAD_DIM = 64
SSD_INNER = SSD_HEADS * SSD_HEAD_DIM
SSD_GROUPS = 2
SSD_STATE = 128
SSD_CONV = 4
SSD_CHUNK = 128
SSD_CONV_CH = SSD_INNER + 2 * SSD_GROUPS * SSD_STATE
MLA_HEADS = 16
MLA_Q_RANK = 384
MLA_KV_RANK = 256
MLA_NOPE = 64
MLA_ROPE = 32
MLA_V = 64
MLA_QK = MLA_NOPE + MLA_ROPE
ROPE_THETA = 10000.0
XA_HEADS = 4
XA_HEAD_DIM = D_MODEL // XA_HEADS
D_FF = 2816
FFN_RES_WEIGHT = 0.5
EPS = 1e-6

LANES = 128
BF16_ROWS = 16
HEAD_PAD = LANES
ROPE_HALF = MLA_ROPE // 2
GROUP_W = SSD_INNER // SSD_GROUPS
PAIRS_PER_GROUP = GROUP_W // LANES
LOG2E = math.log2(math.e)

TM = 1024
FFN_CHUNK = 256
TS = 512
SSD_SEQS = 2
TQ = 256
ATTN_LOOKAHEAD = 3
HIST = BF16_ROWS
VMEM_LIMIT = 56 * 1024 * 1024

_C_Z = (0, SSD_INNER)
_C_XBC = (_C_Z[1], _C_Z[1] + SSD_CONV_CH)
_C_GATE = (_C_XBC[1], _C_XBC[1] + 2 * D_MODEL)
_C_QC = (_C_GATE[1], _C_GATE[1] + MLA_Q_RANK)
_C_KVC = (_C_QC[1], _C_QC[1] + MLA_KV_RANK)
_C_KR = (_C_KVC[1], _C_KVC[1] + HEAD_PAD)
_C_DT = (_C_KR[1], _C_KR[1] + LANES)
D_IN_ARR = _C_DT[1]

_ROWS_DT, _ROWS_ECS, _ROWS_DSF = (0, 32), (32, 64), (64, 96)
_ROWS_CS = (0, 48)

_HALVES = (slice(0, TM // 2), slice(TM // 2, TM))


def _rms(x, g):
    return x * lax.rsqrt(jnp.mean(x * x, axis=-1, keepdims=True) + EPS) * g


def _softplus(x):
    return jnp.maximum(x, 0.0) + jnp.log1p(jnp.exp(-jnp.abs(x)))


def _bf16_pieces(x, n):
    pieces, rest = [], x
    for _ in range(n):
        piece = rest.astype(BF16).astype(F32)
        pieces.append(piece)
        rest = rest - piece
    return pieces


def _resident(shape):
    zeros = (0,) * len(shape)
    return pl.BlockSpec(shape, lambda *_: zeros, pipeline_mode=pl.Buffered(1))


def _params(*sem):
    return pltpu.CompilerParams(dimension_semantics=sem, vmem_limit_bytes=VMEM_LIMIT)


def _ffn_kernel(x_ref, pre_g_ref, wg_ref, wu_ref, wd_ref, post_g_ref, o_ref, h_ref):
    xs = [x_ref[r, :] for r in _HALVES]
    xns = [_rms(x, pre_g_ref[...]).astype(BF16) for x in xs]
    for c in range(D_FF // FFN_CHUNK):
        sl = slice(c * FFN_CHUNK, (c + 1) * FFN_CHUNK)
        for r, xn in zip(_HALVES, xns):
            g = jnp.dot(xn, wg_ref[:, sl], preferred_element_type=F32)
            u = jnp.dot(xn, wu_ref[:, sl], preferred_element_type=F32)
            h_ref[r, sl] = (jax.nn.silu(g) * u).astype(BF16)
    ys = [jnp.dot(h_ref[r, :], wd_ref[...], preferred_element_type=F32) for r in _HALVES]
    for r, x, y in zip(_HALVES, xs, ys):
        o_ref[r, :] = x + FFN_RES_WEIGHT * _rms(y, post_g_ref[...])


def _ffn(x, pre_g, wg, wu, wd, post_g):
    t = x.shape[0]
    tok = pl.BlockSpec((TM, D_MODEL), lambda i: (i, 0))
    return pl.pallas_call(
        _ffn_kernel,
        grid=(t // TM,),
        in_specs=[tok, _resident((1, D_MODEL)), _resident((D_MODEL, D_FF)), _resident((D_MODEL, D_FF)),
                  _resident((D_FF, D_MODEL)), _resident((1, D_MODEL))],
        out_specs=tok,
        out_shape=jax.ShapeDtypeStruct((t, D_MODEL), F32),
        scratch_shapes=[pltpu.VMEM((TM, D_FF), BF16)],
        compiler_params=_params("parallel"),
        name="ffn",
    )(x, pre_g, wg, wu, wd, post_g)


def _inproj_kernel(x_ref, g_ref, w_ref, z_ref, xbc_ref, gate_ref, qc_ref, kvc_ref, kr_ref, dt_ref):
    hns = [_rms(x_ref[r, :], g_ref[...]).astype(BF16) for r in _HALVES]
    outs = ((z_ref, _C_Z), (xbc_ref, _C_XBC), (gate_ref, _C_GATE), (qc_ref, _C_QC), (kvc_ref, _C_KVC),
            (kr_ref, _C_KR), (dt_ref, _C_DT))
    for out_ref, cols in outs:
        for r, hn in zip(_HALVES, hns):
            out_ref[r, :] = jnp.dot(hn, w_ref[:, cols[0]:cols[1]],
                                    preferred_element_type=F32).astype(out_ref.dtype)


def _inproj(x, g, w_arr):
    t = x.shape[0]

    def tok(width):
        return pl.BlockSpec((TM, width), lambda i: (i, 0))

    widths = (SSD_INNER, SSD_CONV_CH, 2 * D_MODEL, MLA_Q_RANK, MLA_KV_RANK, HEAD_PAD, LANES)
    dtypes = (BF16, BF16, BF16, BF16, BF16, F32, F32)
    return pl.pallas_call(
        _inproj_kernel,
        grid=(t // TM,),
        in_specs=[tok(D_MODEL), _resident((1, D_MODEL)), _resident((D_MODEL, D_IN_ARR))],
        out_specs=[tok(w) for w in widths],
        out_shape=[jax.ShapeDtypeStruct((t, w), d) for w, d in zip(widths, dtypes)],
        compiler_params=_params("parallel"),
        name="inproj",
    )(x, g, w_arr)


def _ssd_constants():
    L = SSD_CHUNK
    shift = np.zeros(((SSD_CONV - 1) * L, L + HIST), np.float32)
    for k in range(SSD_CONV - 1):
        shift[k * L + np.arange(L), np.arange(L) + HIST - (SSD_CONV - 1) + k] = 1.0
    head_of_lane = np.arange(SSD_INNER) // SSD_HEAD_DIM
    exp_ch = np.zeros((LANES, 3 * SSD_INNER), np.float32)
    for q, (lo, hi) in enumerate((_ROWS_DT, _ROWS_ECS, _ROWS_DSF)):
        for r in range(lo, hi):
            exp_ch[r, q * SSD_INNER + np.nonzero(head_of_lane == r % SSD_HEADS)[0]] = 1.0
    exp_blk = np.zeros((LANES, SSD_HEADS * LANES), np.float32)
    for r in range(*_ROWS_CS):
        h = r % SSD_HEADS
        exp_blk[r, h * LANES:(h + 1) * LANES] = 1.0
    tri = np.tile((np.arange(L)[:, None] <= np.arange(L)[None, :]).astype(np.float32), (3, 1))
    return tuple(jnp.asarray(m, BF16) for m in (shift, exp_ch, exp_blk, tri))


def _ssd_kernel(xbc_ref, z_ref, dt_ref, shift_ref, expch_ref, expblk_ref, tri_ref, cw_ref, cb_ref, dtb_ref,
                alog_ref, dskip_ref, ng_ref, y_ref, xpad, state):
    L = SSD_CHUNK

    @pl.when(pl.program_id(1) == 0)
    def _start_of_sequence():
        xpad[:, 0:HIST, :] = jnp.zeros((SSD_SEQS, HIST, SSD_CONV_CH), BF16)
        state[...] = jnp.zeros_like(state)

    xpad[:, HIST:HIST + TS, :] = xbc_ref[...]

    low_half = lax.broadcasted_iota(jnp.int32, (L, LANES), 1) < SSD_HEAD_DIM
    tril = lax.broadcasted_iota(jnp.int32, (L, L), 1) <= lax.broadcasted_iota(jnp.int32, (L, L), 0)
    a_neg = -jnp.exp(alog_ref[...])

    def to_token_rows(stack):
        full = jnp.concatenate([stack, jnp.zeros((LANES - stack.shape[0], L), F32)], axis=0)
        return full.T.astype(BF16)

    def chunk_stages(s, c):
        r0 = pl.multiple_of(c * L, L)
        rows = pl.ds(r0, L)
        win = xpad[s, pl.ds(r0, L + HIST), :]
        taps = jnp.dot(shift_ref[...], win, preferred_element_type=F32)
        yield
        dt_t = _softplus(dt_ref[s, rows, :].T[:SSD_HEADS, :] + dtb_ref[...])
        adt_t = dt_t * a_neg
        cs_t = jnp.dot(jnp.concatenate(_bf16_pieces(adt_t, 3), axis=1).astype(BF16), tri_ref[...],
                       preferred_element_type=F32)
        yield
        ecs_t = jnp.exp(cs_t)
        dsf_t = jnp.exp(cs_t[:, L - 1:L] - cs_t)
        stack = jnp.concatenate(_bf16_pieces(dt_t, 2) + _bf16_pieces(ecs_t, 2) + _bf16_pieces(dsf_t, 2), axis=0)
        per_ch = jnp.dot(to_token_rows(stack), expch_ref[...], preferred_element_type=F32)
        dt_e = per_ch[:, :SSD_INNER]
        ecs_e = per_ch[:, SSD_INNER:2 * SSD_INNER]
        dsf_e = per_ch[:, 2 * SSD_INNER:]
        cs_col = jnp.dot(to_token_rows(jnp.concatenate(_bf16_pieces(cs_t, 3), axis=0)), expblk_ref[...],
                         preferred_element_type=F32)
        yield
        u = cb_ref[...] + cw_ref[SSD_CONV - 1:SSD_CONV, :] * win[HIST:, :].astype(F32)
        for k in range(SSD_CONV - 1):
            u = u + cw_ref[k:k + 1, :] * taps[k * L:(k + 1) * L, :]
        u = jax.nn.silu(u)
        xs = u[:, :SSD_INNER]
        bmat = u[:, SSD_INNER:SSD_INNER + SSD_GROUPS * SSD_STATE]
        cmat = u[:, SSD_INNER + SSD_GROUPS * SSD_STATE:]
        z = z_ref[s, rows, :].astype(F32)
        yield

        for g in range(SSD_GROUPS):
            gcols = slice(g * GROUP_W, (g + 1) * GROUP_W)
            bm_g = bmat[:, g * SSD_STATE:(g + 1) * SSD_STATE]
            cm_g = cmat[:, g * SSD_STATE:(g + 1) * SSD_STATE].astype(BF16)
            cbm = lax.dot_general(cm_g, bm_g.astype(BF16), (((1,), (1,)), ((), ())),
                                  preferred_element_type=F32)
            s_g = state[s, g]
            y_off = jnp.dot(cm_g, s_g.astype(BF16), preferred_element_type=F32)
            xs_g = xs[:, gcols]
            xdt_g = xs_g * dt_e[:, gcols]
            xw_g = (xdt_g * dsf_e[:, gcols]).astype(BF16)
            state[s, g] = (s_g * ecs_e[L - 1:L, gcols]
                           + jnp.dot(bm_g.T.astype(BF16), xw_g, preferred_element_type=F32))
            yield
            y_parts = []
            for pp in range(PAIRS_PER_GROUP):
                p = g * PAIRS_PER_GROUP + pp

                def masked(h):
                    dec = jnp.exp(jnp.where(tril, cs_col[:, h * LANES:(h + 1) * LANES] - cs_t[h:h + 1, :],
                                            -jnp.inf))
                    return (cbm * dec).astype(BF16)

                xdt_p = xdt_g[:, pp * LANES:(pp + 1) * LANES]
                m2 = jnp.concatenate([masked(2 * p), masked(2 * p + 1)], axis=1)
                x_blk = jnp.concatenate([jnp.where(low_half, xdt_p, 0.0), jnp.where(low_half, 0.0, xdt_p)],
                                        axis=0).astype(BF16)
                y_parts.append(jnp.dot(m2, x_blk, preferred_element_type=F32))
                yield
            y_g = (jnp.concatenate(y_parts, axis=1) + y_off * ecs_e[:, gcols] + dskip_ref[:, gcols] * xs_g)
            y_g = y_g * jax.nn.silu(z[:, gcols])
            y_ref[s, rows, gcols] = _rms(y_g, ng_ref[:, gcols]).astype(BF16)
            yield

    def chunk(c, carry):
        for _ in zip(*[chunk_stages(s, c) for s in range(SSD_SEQS)]):
            pass
        return carry

    lax.fori_loop(0, TS // L, chunk, 0)
    xpad[:, 0:HIST, :] = xpad[:, TS:TS + HIST, :]


def _ssd(xbc, z, dt, conv_w, conv_b, dt_bias_t, a_log_t, d_skip_exp, norm_g, batch, seq):
    t = xbc.shape[0]
    nblk = seq // TS
    consts = _ssd_constants()

    def tok(width):
        return pl.BlockSpec((SSD_SEQS, TS, width), lambda b, j: (b, j, 0))

    by_seq = lambda a: a.reshape(batch, seq, a.shape[-1])
    y = pl.pallas_call(
        _ssd_kernel,
        grid=(batch // SSD_SEQS, nblk),
        in_specs=[tok(SSD_CONV_CH), tok(SSD_INNER), tok(LANES)] + [_resident(m.shape) for m in consts] + [
            _resident((SSD_CONV, SSD_CONV_CH)), _resident((1, SSD_CONV_CH)),
            _resident((SSD_HEADS, SSD_CHUNK)), _resident((SSD_HEADS, SSD_CHUNK)),
            _resident((1, SSD_INNER)), _resident((1, SSD_INNER))],
        out_specs=tok(SSD_INNER),
        out_shape=jax.ShapeDtypeStruct((batch, seq, SSD_INNER), BF16),
        scratch_shapes=[pltpu.VMEM((SSD_SEQS, TS + HIST, SSD_CONV_CH), BF16),
                        pltpu.VMEM((SSD_SEQS, SSD_GROUPS, SSD_STATE, GROUP_W), F32)],
        compiler_params=_params("parallel", "arbitrary"),
        name="ssd",
    )(by_seq(xbc), by_seq(z), by_seq(dt), *consts, conv_w, conv_b, dt_bias_t, a_log_t, d_skip_exp, norm_g)
    return y.reshape(t, SSD_INNER)


def _rope_constants():
    e = np.zeros((LANES, 2 * HEAD_PAD), np.float32)
    for piece in range(3):
        for j in range(ROPE_HALF):
            c_row = piece * ROPE_HALF + j
            s_row = 3 * ROPE_HALF + piece * ROPE_HALF + j
            e[c_row, MLA_NOPE + j] = e[c_row, MLA_NOPE + ROPE_HALF + j] = 1.0
            e[s_row, MLA_QK + j] = e[s_row, MLA_QK + ROPE_HALF + j] = 1.0
            e[s_row, HEAD_PAD + MLA_NOPE + j] = -1.0
            e[s_row, HEAD_PAD + MLA_NOPE + ROPE_HALF + j] = 1.0
    return jnp.asarray(e, BF16)


def _mla_prep_kernel(qc_ref, kvc_ref, kr_ref, pos_ref, inv_ref, erope_ref, qg_ref, kvg_ref, wuq_ref, wuk_ref,
                     wuv_ref, vones_ref, q_out, k_out, v_out):
    ang_t = inv_ref[...] * pos_ref[0].astype(F32)
    stack = jnp.concatenate(_bf16_pieces(jnp.cos(ang_t), 3) + _bf16_pieces(jnp.sin(ang_t), 3)
                            + [jnp.zeros((LANES - 6 * ROPE_HALF, TM), F32)], axis=0)
    tabs = jnp.dot(stack.T.astype(BF16), erope_ref[...], preferred_element_type=F32)
    q_raw = tabs[:, :HEAD_PAD]
    k_sin = tabs[:, HEAD_PAD:]
    lane = lax.broadcasted_iota(jnp.int32, q_raw.shape, 1)
    first = (lane >= MLA_NOPE) & (lane < MLA_NOPE + ROPE_HALF)
    q_tab = jnp.where(lane < MLA_NOPE, 1.0, q_raw) * (MLA_QK ** -0.5 * LOG2E)

    kr = kr_ref[...]
    swapped = jnp.where(first, pltpu.roll(kr, HEAD_PAD - ROPE_HALF, 1), pltpu.roll(kr, ROPE_HALF, 1))
    roped = kr * q_raw + swapped * k_sin
    k_rope = roped + pltpu.roll(roped, MLA_ROPE, 1)

    qn = _rms(qc_ref[...].astype(F32), qg_ref[...]).astype(BF16)
    kvn = _rms(kvc_ref[...].astype(F32), kvg_ref[...]).astype(BF16)
    for hp in range(MLA_HEADS // 2):
        cols2 = slice(2 * hp * HEAD_PAD, (2 * hp + 2) * HEAD_PAD)
        q2 = jnp.dot(qn, wuq_ref[:, cols2], preferred_element_type=F32)
        k2 = jnp.dot(kvn, wuk_ref[:, cols2], preferred_element_type=F32)
        v2 = jnp.dot(kvn, wuv_ref[:, cols2], preferred_element_type=F32)
        for hh in range(2):
            cols = slice((2 * hp + hh) * HEAD_PAD, (2 * hp + hh + 1) * HEAD_PAD)
            half = slice(hh * HEAD_PAD, (hh + 1) * HEAD_PAD)
            q_out[:, cols] = (q2[:, half] * q_tab).astype(BF16)
            k_out[:, cols] = (k2[:, half] + k_rope).astype(BF16)
        v_out[:, cols2] = (v2 + vones_ref[:, cols2]).astype(BF16)


def _mla_prep(qc, kvc, kr, pos_rows, inv_col, q_g, kv_g, wuq_pad, wuk_pad, wuv_pad, v_ones):
    t = qc.shape[0]
    e_rope = _rope_constants()

    def tok(width):
        return pl.BlockSpec((TM, width), lambda i: (i, 0))

    wide = MLA_HEADS * HEAD_PAD
    return pl.pallas_call(
        _mla_prep_kernel,
        grid=(t // TM,),
        in_specs=[tok(MLA_Q_RANK), tok(MLA_KV_RANK), tok(HEAD_PAD), pl.BlockSpec((1, 1, TM), lambda i: (i, 0, 0)),
                  _resident((ROPE_HALF, 1)), _resident(e_rope.shape),
                  _resident((1, MLA_Q_RANK)), _resident((1, MLA_KV_RANK)),
                  _resident((MLA_Q_RANK, wide)), _resident((MLA_KV_RANK, wide)), _resident((MLA_KV_RANK, wide)),
                  _resident((1, wide))],
        out_specs=[tok(wide)] * 3,
        out_shape=[jax.ShapeDtypeStruct((t, wide), BF16)] * 3,
        compiler_params=_params("parallel"),
        name="mla_prep",
    )(qc, kvc, kr, pos_rows, inv_col, e_rope, q_g, kv_g, wuq_pad, wuk_pad, wuv_pad, v_ones)


def _attn_kernel(q_ref, k_ref, v_ref, o_ref):
    seq = q_ref.shape[0]
    causal = (lax.broadcasted_iota(jnp.int32, (TQ, TQ), 1) <= lax.broadcasted_iota(jnp.int32, (TQ, TQ), 0))
    low_half = lax.broadcasted_iota(jnp.int32, (TQ, HEAD_PAD), 1) < MLA_V
    nt = (((1,), (1,)), ((), ()))
    units = [(i, hh) for i in reversed(range(seq // TQ)) for hh in range(2)]

    def scores(i, hh):
        rows = slice(i * TQ, (i + 1) * TQ)
        hcols = slice(hh * HEAD_PAD, (hh + 1) * HEAD_PAD)
        q = q_ref[rows, hcols]
        s_diag = jnp.where(causal, lax.dot_general(q, k_ref[rows, hcols], nt, preferred_element_type=F32),
                           -jnp.inf)
        s_past = lax.dot_general(q, k_ref[0:i * TQ, hcols], nt, preferred_element_type=F32) if i else None
        return s_diag, s_past

    def weighted_values(i, hh, s_diag, s_past):
        rows = slice(i * TQ, (i + 1) * TQ)
        hcols = slice(hh * HEAD_PAD, (hh + 1) * HEAD_PAD)
        m = jnp.max(s_diag, axis=1, keepdims=True)
        if i:
            m = jnp.maximum(m, jnp.max(s_past, axis=1, keepdims=True))
        acc = jnp.dot(jnp.exp2(s_diag - m).astype(BF16), v_ref[rows, hcols], preferred_element_type=F32)
        if i:
            acc = acc + jnp.dot(jnp.exp2(s_past - m).astype(BF16), v_ref[0:i * TQ, hcols],
                                preferred_element_type=F32)
        return acc

    pending = [scores(*unit) for unit in units[:ATTN_LOOKAHEAD]]
    accs = []
    for n, (i, hh) in enumerate(units):
        current = pending.pop(0)
        if n + ATTN_LOOKAHEAD < len(units):
            pending.append(scores(*units[n + ATTN_LOOKAHEAD]))
        accs.append(weighted_values(i, hh, *current))
        if hh == 1:
            num = jnp.where(low_half, accs[0], accs[1])
            den = pltpu.roll(jnp.where(low_half, accs[1], accs[0]), MLA_V, 1)
            o_ref[i * TQ:(i + 1) * TQ, :] = (num / den).astype(BF16)
            accs = []


def _attention(qf, kf, vf, batch, seq):
    t = qf.shape[0]
    pair = pl.BlockSpec((seq, 2 * HEAD_PAD), lambda b, hp: (b, hp))
    return pl.pallas_call(
        _attn_kernel,
        grid=(batch, MLA_HEADS // 2),
        in_specs=[pair, pair, pair],
        out_specs=pl.BlockSpec((seq, 2 * MLA_V), lambda b, hp: (b, hp)),
        out_shape=jax.ShapeDtypeStruct((t, MLA_HEADS * MLA_V), BF16),
        compiler_params=_params("parallel", "parallel"),
        name="mla_attention",
    )(qf, kf, vf)


def _merge_kernel(x_ref, yn_ref, o_ref, gate_ref, gb_ref, wssd_ref, wmla_ref, wout_ref, post_g_ref, out_ref):
    mixed = []
    for r in _HALVES:
        y_ssd = jnp.dot(yn_ref[r, :], wssd_ref[...], preferred_element_type=F32)
        y_mla = jnp.dot(o_ref[r, :], wmla_ref[...], preferred_element_type=F32)
        gates = jax.nn.sigmoid(gate_ref[r, :].astype(F32) + gb_ref[...])
        mixed.append((gates[:, :D_MODEL] * y_ssd + gates[:, D_MODEL:] * y_mla).astype(BF16))
    hs = [jnp.dot(m, wout_ref[...], preferred_element_type=F32) for m in mixed]
    for r, h in zip(_HALVES, hs):
        out_ref[r, :] = x_ref[r, :] + _rms(h, post_g_ref[...])


def _merge(x, yn, o, gate, gate_bias, wssd, wmla, wout, post_g):
    t = x.shape[0]

    def tok(width):
        return pl.BlockSpec((TM, width), lambda i: (i, 0))

    sq = _resident((D_MODEL, D_MODEL))
    return pl.pallas_call(
        _merge_kernel,
        grid=(t // TM,),
        in_specs=[tok(D_MODEL), tok(SSD_INNER), tok(MLA_HEADS * MLA_V), tok(2 * D_MODEL),
                  _resident((1, 2 * D_MODEL)), sq, sq, sq, _resident((1, D_MODEL))],
        out_specs=tok(D_MODEL),
        out_shape=jax.ShapeDtypeStruct((t, D_MODEL), F32),
        compiler_params=_params("parallel"),
        name="merge",
    )(x, yn, o, gate, gate_bias, wssd, wmla, wout, post_g)


def _memkv_kernel(mem_ref, g_ref, wk_ref, wv_ref, k_ref, v_ref):
    mn = _rms(mem_ref[...], g_ref[...]).astype(BF16)
    k_ref[...] = jnp.dot(mn, wk_ref[...], preferred_element_type=F32).astype(BF16)
    v_ref[...] = jnp.dot(mn, wv_ref[...], preferred_element_type=F32).astype(BF16)


def _memkv(mem, g, wk, wv):
    t = mem.shape[0]
    rows = math.gcd(TM, t)
    blk = pl.BlockSpec((rows, D_MODEL), lambda b: (b, 0))
    sq = _resident((D_MODEL, D_MODEL))
    return pl.pallas_call(
        _memkv_kernel,
        grid=(t // rows,),
        in_specs=[blk, _resident((1, D_MODEL)), sq, sq],
        out_specs=[blk, blk],
        out_shape=[jax.ShapeDtypeStruct((t, D_MODEL), BF16)] * 2,
        compiler_params=_params("parallel"),
        name="memkv",
    )(mem, g, wk, wv)


def _xattn_kernel(x_ref, g_ref, wq_ref, k_ref, v_ref, wo_ref, post_g_ref, out_ref):
    heads = [slice(h * XA_HEAD_DIM, (h + 1) * XA_HEAD_DIM) for h in range(XA_HEADS)]
    nt = (((1,), (1,)), ((), ()))
    xs = [x_ref[r, :] for r in _HALVES]
    hns = [_rms(x, g_ref[...]).astype(BF16) for x in xs]
    qs = [(jnp.dot(hn, wq_ref[...], preferred_element_type=F32) * (XA_HEAD_DIM ** -0.5 * LOG2E)).astype(BF16)
          for hn in hns]
    scores = [[lax.dot_general(q[:, cols], k_ref[:, cols], nt, preferred_element_type=F32) for cols in heads]
              for q in qs]
    attended = []
    for head_scores in scores:
        outs = []
        for cols, s in zip(heads, head_scores):
            p = jnp.exp2(s - jnp.max(s, axis=1, keepdims=True))
            l = jnp.sum(p, axis=1, keepdims=True)
            o = jnp.dot(p.astype(BF16), v_ref[:, cols], preferred_element_type=F32) / l
            outs.append(o.astype(BF16))
        attended.append(jnp.concatenate(outs, axis=1))
    ys = [jnp.dot(o, wo_ref[...], preferred_element_type=F32) for o in attended]
    for r, x, y in zip(_HALVES, xs, ys):
        out_ref[r, :] = x + _rms(y, post_g_ref[...])


def _xattn(x, g, wq, k, v, wo, post_g, batch, seq, mem_len):
    t = x.shape[0]
    nblk = seq // TM
    tok = pl.BlockSpec((TM, D_MODEL), lambda b, j: (b * nblk + j, 0))
    kv = pl.BlockSpec((mem_len, D_MODEL), lambda b, j: (b, 0))
    sq = _resident((D_MODEL, D_MODEL))
    return pl.pallas_call(
        _xattn_kernel,
        grid=(batch, nblk),
        in_specs=[tok, _resident((1, D_MODEL)), sq, kv, kv, sq, _resident((1, D_MODEL))],
        out_specs=tok,
        out_shape=jax.ShapeDtypeStruct((t, D_MODEL), F32),
        compiler_params=_params("parallel", "arbitrary"),
        name="xattn",
    )(x, g, wq, k, v, wo, post_g)


def _row(v):
    return v.reshape(1, -1).astype(F32)


def _pad_cols(w, left, right):
    return jnp.pad(w, ((0, 0), (left, right)))


def _layer(x, mem, pos, l, p, batch, seq, mem_len):
    bf = lambda w: w.astype(BF16)

    x = _ffn(x, _row(p["ffn1_pre_g"][l]), bf(p["ffn1_w_gate"][l]), bf(p["ffn1_w_up"][l]),
             bf(p["ffn1_w_down"][l]), _row(p["ffn1_post_g"][l]))

    w_in = p["w_in"][l]
    o_z = 0
    o_xbc = o_z + SSD_INNER
    o_dt = o_xbc + SSD_CONV_CH
    o_qc = o_dt + SSD_HEADS
    o_kvc = o_qc + MLA_Q_RANK
    o_kr = o_kvc + MLA_KV_RANK
    o_gate = o_kr + MLA_ROPE
    w_arr = jnp.concatenate([
        w_in[:, o_z:o_xbc], w_in[:, o_xbc:o_dt], w_in[:, o_gate:], w_in[:, o_qc:o_kvc], w_in[:, o_kvc:o_kr],
        _pad_cols(w_in[:, o_kr:o_gate], MLA_NOPE, HEAD_PAD - MLA_QK),
        _pad_cols(w_in[:, o_dt:o_qc], 0, LANES - SSD_HEADS)], axis=1)
    z, xbc, gate, qc, kvc, kr, dt = _inproj(x, _row(p["mix_pre_g"][l]), bf(w_arr))

    d_skip_exp = jnp.repeat(p["d_skip"][l].astype(F32), SSD_HEAD_DIM).reshape(1, SSD_INNER)
    per_head_rows = lambda v: jnp.broadcast_to(v.astype(F32)[:, None], (SSD_HEADS, SSD_CHUNK))
    yn = _ssd(xbc, z, dt, p["conv_w"][l].astype(F32), _row(p["conv_b"][l]), per_head_rows(p["dt_bias"][l]),
              per_head_rows(p["a_log"][l]), d_skip_exp, _row(p["ssd_norm_g"][l]), batch, seq)

    inv = ROPE_THETA ** (-jnp.arange(0, MLA_ROPE, 2, dtype=F32) / MLA_ROPE)
    inv_col = inv.reshape(ROPE_HALF, 1)
    wq = p["w_uq"][l].reshape(MLA_Q_RANK, MLA_HEADS, MLA_QK)
    q_x1 = wq[..., MLA_NOPE:MLA_NOPE + ROPE_HALF]
    q_x2 = wq[..., MLA_NOPE + ROPE_HALF:]
    wuq_pad = jnp.concatenate([wq, -q_x2, q_x1], axis=-1).reshape(MLA_Q_RANK, MLA_HEADS * HEAD_PAD)
    wuk_pad = jnp.pad(p["w_uk"][l].reshape(MLA_KV_RANK, MLA_HEADS, MLA_NOPE),
                      ((0, 0), (0, 0), (0, HEAD_PAD - MLA_NOPE))).reshape(MLA_KV_RANK, MLA_HEADS * HEAD_PAD)
    wv = p["w_uv"][l].reshape(MLA_KV_RANK, MLA_HEADS // 2, 2, MLA_V)
    zero_v = jnp.zeros_like(wv[:, :, 0])
    wuv_pad = jnp.stack([jnp.concatenate([wv[:, :, 0], zero_v], axis=-1),
                         jnp.concatenate([zero_v, wv[:, :, 1]], axis=-1)], axis=2)
    wuv_pad = wuv_pad.reshape(MLA_KV_RANK, MLA_HEADS * HEAD_PAD)
    v_ones = jnp.tile(jnp.concatenate([jnp.zeros((MLA_V,), F32), jnp.ones((2 * MLA_V,), F32),
                                       jnp.zeros((MLA_V,), F32)]), MLA_HEADS // 2).reshape(1, MLA_HEADS * HEAD_PAD)
    qf, kf, vf = _mla_prep(qc, kvc, kr, pos, inv_col, _row(p["q_norm_g"][l]), _row(p["kv_norm_g"][l]),
                           bf(wuq_pad), bf(wuk_pad), bf(wuv_pad), v_ones)
    o = _attention(qf, kf, vf, batch, seq)

    x = _merge(x, yn, o, gate, _row(p["gate_bias"][l]), bf(p["w_ssd_proj"][l]), bf(p["w_mla_proj"][l]),
               bf(p["w_out"][l]), _row(p["mix_post_g"][l]))

    mk, mv = _memkv(mem, _row(p["mem_norm_g"][l]), bf(p["w_xk"][l]), bf(p["w_xv"][l]))
    x = _xattn(x, _row(p["xa_pre_g"][l]), bf(p["w_xq"][l]), mk, mv, bf(p["w_xo"][l]), _row(p["xa_post_g"][l]),
               batch, seq, mem_len)

    return _ffn(x, _row(p["ffn2_pre_g"][l]), bf(p["ffn2_w_gate"][l]), bf(p["ffn2_w_up"][l]),
                bf(p["ffn2_w_down"][l]), _row(p["ffn2_post_g"][l]))


def kernel(x, mem, positions, ffn1_pre_g, ffn1_w_gate, ffn1_w_up, ffn1_w_down, ffn1_post_g, mix_pre_g, w_in, conv_w, conv_b, dt_bias, a_log, d_skip, ssd_norm_g, w_ssd_proj, q_norm_g, w_uq, kv_norm_g, w_uk, w_uv, w_mla_proj, gate_bias, w_out, mix_post_g, xa_pre_g, mem_norm_g, w_xq, w_xk, w_xv, w_xo, xa_post_g, ffn2_pre_g, ffn2_w_gate, ffn2_w_up, ffn2_w_down, ffn2_post_g):
    p = dict(ffn1_pre_g=ffn1_pre_g, ffn1_w_gate=ffn1_w_gate, ffn1_w_up=ffn1_w_up, ffn1_w_down=ffn1_w_down,
             ffn1_post_g=ffn1_post_g, mix_pre_g=mix_pre_g, w_in=w_in, conv_w=conv_w, conv_b=conv_b,
             dt_bias=dt_bias, a_log=a_log, d_skip=d_skip, ssd_norm_g=ssd_norm_g, w_ssd_proj=w_ssd_proj,
             q_norm_g=q_norm_g, w_uq=w_uq, kv_norm_g=kv_norm_g, w_uk=w_uk, w_uv=w_uv, w_mla_proj=w_mla_proj,
             gate_bias=gate_bias, w_out=w_out, mix_post_g=mix_post_g, xa_pre_g=xa_pre_g, mem_norm_g=mem_norm_g,
             w_xq=w_xq, w_xk=w_xk, w_xv=w_xv, w_xo=w_xo, xa_post_g=xa_post_g, ffn2_pre_g=ffn2_pre_g,
             ffn2_w_gate=ffn2_w_gate, ffn2_w_up=ffn2_w_up, ffn2_w_down=ffn2_w_down, ffn2_post_g=ffn2_post_g)
    batch, seq, _ = x.shape
    mem_len = mem.shape[1]
    assert seq % TS == 0 and seq % TM == 0 and seq % TQ == 0 and TS % SSD_CHUNK == 0 and batch % SSD_SEQS == 0
    xf = x.reshape(batch * seq, D_MODEL)
    memf = mem.reshape(batch * mem_len, D_MODEL)
    pos = positions.reshape(batch * seq // TM, 1, TM)
    for l in range(w_in.shape[0]):
        xf = _layer(xf, memf, pos, l, p, batch, seq, mem_len)
    return xf.reshape(batch, seq, D_MODEL)
```

```python
import math

import numpy as np
import jax
import jax.numpy as jnp
from jax import lax
from jax.experimental import pallas as pl
from jax.experimental.pallas import tpu as pltpu

F32 = jnp.float32
BF16 = jnp.bfloat16

D_MODEL = 1024
SSD_HEADS = 16
SSD_HEAD_DIM = 64
SSD_INNER = SSD_HEADS * SSD_HEAD_DIM
SSD_GROUPS = 2
SSD_STATE = 128
SSD_CONV = 4
SSD_CHUNK = 128
SSD_CONV_CH = SSD_INNER + 2 * SSD_GROUPS * SSD_STATE
MLA_HEADS = 16
MLA_Q_RANK = 384
MLA_KV_RANK = 256
MLA_NOPE = 64
MLA_ROPE = 32
MLA_V = 64
MLA_QK = MLA_NOPE + MLA_ROPE
ROPE_THETA = 10000.0
XA_HEADS = 4
XA_HEAD_DIM = D_MODEL // XA_HEADS
D_FF = 2816
FFN_RES_WEIGHT = 0.5
EPS = 1e-6

LANES = 128
BF16_ROWS = 16
HEAD_PAD = LANES
ROPE_HALF = MLA_ROPE // 2
GROUP_W = SSD_INNER // SSD_GROUPS
PAIRS_PER_GROUP = GROUP_W // LANES
LOG2E = math.log2(math.e)

TM = 1024
FFN_CHUNK = 256
TS = 512
SSD_SEQS = 4
TQ = 256
ATTN_LOOKAHEAD = 3
HIST = BF16_ROWS
VMEM_LIMIT = 56 * 1024 * 1024

_C_Z = (0, SSD_INNER)
_C_XBC = (_C_Z[1], _C_Z[1] + SSD_CONV_CH)
_C_GATE = (_C_XBC[1], _C_XBC[1] + 2 * D_MODEL)
_C_QC = (_C_GATE[1], _C_GATE[1] + MLA_Q_RANK)
_C_KVC = (_C_QC[1], _C_QC[1] + MLA_KV_RANK)
_C_DTKR = (_C_KVC[1], _C_KVC[1] + LANES)
D_IN_ARR = _C_DTKR[1]

_ROWS_DT, _ROWS_ECS, _ROWS_DSF = (0, 32), (32, 64), (64, 96)
_ROWS_CS = (0, 48)

_HALVES = (slice(0, TM // 2), slice(TM // 2, TM))


def _rms(x, g):
    return x * lax.rsqrt(jnp.mean(x * x, axis=-1, keepdims=True) + EPS) * g


def _softplus(x):
    return jnp.maximum(x, 0.0) + jnp.log1p(jnp.exp(-jnp.abs(x)))


def _bf16_pieces(x, n):
    pieces, rest = [], x
    for _ in range(n):
        piece = rest.astype(BF16).astype(F32)
        pieces.append(piece)
        rest = rest - piece
    return pieces


def _resident(shape):
    zeros = (0,) * len(shape)
    return pl.BlockSpec(shape, lambda *_: zeros, pipeline_mode=pl.Buffered(1))


def _params(*sem):
    return pltpu.CompilerParams(dimension_semantics=sem, vmem_limit_bytes=VMEM_LIMIT)


def _ffn_kernel(x_ref, pre_g_ref, wg_ref, wu_ref, wd_ref, post_g_ref, o_ref, h_ref):
    xs = [x_ref[r, :] for r in _HALVES]
    xns = [_rms(x, pre_g_ref[...]).astype(BF16) for x in xs]
    for c in range(D_FF // FFN_CHUNK):
        sl = slice(c * FFN_CHUNK, (c + 1) * FFN_CHUNK)
        for r, xn in zip(_HALVES, xns):
            g = jnp.dot(xn, wg_ref[:, sl], preferred_element_type=F32)
            u = jnp.dot(xn, wu_ref[:, sl], preferred_element_type=F32)
            h_ref[r, sl] = (jax.nn.silu(g) * u).astype(BF16)
    ys = [jnp.dot(h_ref[r, :], wd_ref[...], preferred_element_type=F32) for r in _HALVES]
    for r, x, y in zip(_HALVES, xs, ys):
        o_ref[r, :] = x + FFN_RES_WEIGHT * _rms(y, post_g_ref[...])


def _ffn(x, pre_g, wg, wu, wd, post_g):
    t = x.shape[0]
    tok = pl.BlockSpec((TM, D_MODEL), lambda i: (i, 0))
    return pl.pallas_call(
        _ffn_kernel,
        grid=(t // TM,),
        in_specs=[tok, _resident((1, D_MODEL)), _resident((D_MODEL, D_FF)), _resident((D_MODEL, D_FF)),
                  _resident((D_FF, D_MODEL)), _resident((1, D_MODEL))],
        out_specs=tok,
        out_shape=jax.ShapeDtypeStruct((t, D_MODEL), F32),
        scratch_shapes=[pltpu.VMEM((TM, D_FF), BF16)],
        compiler_params=_params("parallel"),
        name="ffn",
    )(x, pre_g, wg, wu, wd, post_g)


def _inproj_kernel(x_ref, g_ref, w_ref, z_ref, xbc_ref, gate_ref, qc_ref, kvc_ref, dtkr_ref):
    hns = [_rms(x_ref[r, :], g_ref[...]).astype(BF16) for r in _HALVES]
    outs = ((z_ref, _C_Z), (xbc_ref, _C_XBC), (gate_ref, _C_GATE), (qc_ref, _C_QC), (kvc_ref, _C_KVC),
            (dtkr_ref, _C_DTKR))
    for out_ref, cols in outs:
        for r, hn in zip(_HALVES, hns):
            out_ref[r, :] = jnp.dot(hn, w_ref[:, cols[0]:cols[1]],
                                    preferred_element_type=F32).astype(out_ref.dtype)


def _inproj(x, g, w_arr):
    t = x.shape[0]

    def tok(width):
        return pl.BlockSpec((TM, width), lambda i: (i, 0))

    widths = (SSD_INNER, SSD_CONV_CH, 2 * D_MODEL, MLA_Q_RANK, MLA_KV_RANK, LANES)
    dtypes = (BF16, BF16, BF16, BF16, BF16, F32)
    return pl.pallas_call(
        _inproj_kernel,
        grid=(t // TM,),
        in_specs=[tok(D_MODEL), _resident((1, D_MODEL)), _resident((D_MODEL, D_IN_ARR))],
        out_specs=[tok(w) for w in widths],
        out_shape=[jax.ShapeDtypeStruct((t, w), d) for w, d in zip(widths, dtypes)],
        compiler_params=_params("parallel"),
        name="inproj",
    )(x, g, w_arr)


def _ssd_constants():
    L = SSD_CHUNK
    shift = np.zeros(((SSD_CONV - 1) * L, L + HIST), np.float32)
    for k in range(SSD_CONV - 1):
        shift[k * L + np.arange(L), np.arange(L) + HIST - (SSD_CONV - 1) + k] = 1.0
    head_of_lane = np.arange(SSD_INNER) // SSD_HEAD_DIM
    exp_ch = np.zeros((LANES, 3 * SSD_INNER), np.float32)
    for q, (lo, hi) in enumerate((_ROWS_DT, _ROWS_ECS, _ROWS_DSF)):
        for r in range(lo, hi):
            exp_ch[r, q * SSD_INNER + np.nonzero(head_of_lane == r % SSD_HEADS)[0]] = 1.0
    exp_blk = np.zeros((LANES, SSD_HEADS * LANES), np.float32)
    for r in range(*_ROWS_CS):
        h = r % SSD_HEADS
        exp_blk[r, h * LANES:(h + 1) * LANES] = 1.0
    tri = np.tile((np.arange(L)[:, None] <= np.arange(L)[None, :]).astype(np.float32), (3, 1))
    return tuple(jnp.asarray(m, BF16) for m in (shift, exp_ch, exp_blk, tri))


def _ssd_kernel(xbc_ref, z_ref, dt_ref, shift_ref, expch_ref, expblk_ref, tri_ref, cw_ref, cb_ref, dtb_ref,
                alog_ref, dskip_ref, ng_ref, y_ref, xpad, state):
    L = SSD_CHUNK

    @pl.when(pl.program_id(1) == 0)
    def _start_of_sequence():
        xpad[:, 0:HIST, :] = jnp.zeros((SSD_SEQS, HIST, SSD_CONV_CH), BF16)
        state[...] = jnp.zeros_like(state)

    xpad[:, HIST:HIST + TS, :] = xbc_ref[...]

    low_half = lax.broadcasted_iota(jnp.int32, (L, LANES), 1) < SSD_HEAD_DIM
    tril = lax.broadcasted_iota(jnp.int32, (L, L), 1) <= lax.broadcasted_iota(jnp.int32, (L, L), 0)
    a_neg = -jnp.exp(alog_ref[...])

    def to_token_rows(stack):
        full = jnp.concatenate([stack, jnp.zeros((LANES - stack.shape[0], L), F32)], axis=0)
        return full.T.astype(BF16)

    def chunk_stages(s, c):
        r0 = pl.multiple_of(c * L, L)
        rows = pl.ds(r0, L)
        win = xpad[s, pl.ds(r0, L + HIST), :]
        taps = jnp.dot(shift_ref[...], win, preferred_element_type=F32)
        yield
        dt_t = _softplus(dt_ref[s, rows, :].T[:SSD_HEADS, :] + dtb_ref[...])
        adt_t = dt_t * a_neg
        cs_t = jnp.dot(jnp.concatenate(_bf16_pieces(adt_t, 3), axis=1).astype(BF16), tri_ref[...],
                       preferred_element_type=F32)
        yield
        ecs_t = jnp.exp(cs_t)
        dsf_t = jnp.exp(cs_t[:, L - 1:L] - cs_t)
        stack = jnp.concatenate(_bf16_pieces(dt_t, 2) + _bf16_pieces(ecs_t, 2) + _bf16_pieces(dsf_t, 2), axis=0)
        per_ch = jnp.dot(to_token_rows(stack), expch_ref[...], preferred_element_type=F32)
        dt_e = per_ch[:, :SSD_INNER]
        ecs_e = per_ch[:, SSD_INNER:2 * SSD_INNER]
        dsf_e = per_ch[:, 2 * SSD_INNER:]
        cs_col = jnp.dot(to_token_rows(jnp.concatenate(_bf16_pieces(cs_t, 3), axis=0)), expblk_ref[...],
                         preferred_element_type=F32)
        yield
        u = cb_ref[...] + cw_ref[SSD_CONV - 1:SSD_CONV, :] * win[HIST:, :].astype(F32)
        for k in range(SSD_CONV - 1):
            u = u + cw_ref[k:k + 1, :] * taps[k * L:(k + 1) * L, :]
        u = jax.nn.silu(u)
        xs = u[:, :SSD_INNER]
        bmat = u[:, SSD_INNER:SSD_INNER + SSD_GROUPS * SSD_STATE]
        cmat = u[:, SSD_INNER + SSD_GROUPS * SSD_STATE:]
        z = z_ref[s, rows, :].astype(F32)
        yield

        for g in range(SSD_GROUPS):
            gcols = slice(g * GROUP_W, (g + 1) * GROUP_W)
            bm_g = bmat[:, g * SSD_STATE:(g + 1) * SSD_STATE]
            cm_g = cmat[:, g * SSD_STATE:(g + 1) * SSD_STATE].astype(BF16)
            cbm = lax.dot_general(cm_g, bm_g.astype(BF16), (((1,), (1,)), ((), ())),
                                  preferred_element_type=F32)
            s_g = state[s, g]
            y_off = jnp.dot(cm_g, s_g.astype(BF16), preferred_element_type=F32)
            xs_g = xs[:, gcols]
            xdt_g = xs_g * dt_e[:, gcols]
            xw_g = (xdt_g * dsf_e[:, gcols]).astype(BF16)
            state[s, g] = (s_g * ecs_e[L - 1:L, gcols]
                           + jnp.dot(bm_g.T.astype(BF16), xw_g, preferred_element_type=F32))
            yield
            y_parts = []
            for pp in range(PAIRS_PER_GROUP):
                p = g * PAIRS_PER_GROUP + pp

                def masked(h):
                    dec = jnp.exp(jnp.where(tril, cs_col[:, h * LANES:(h + 1) * LANES] - cs_t[h:h + 1, :],
                                            -jnp.inf))
                    return (cbm * dec).astype(BF16)

                xdt_p = xdt_g[:, pp * LANES:(pp + 1) * LANES]
                m2 = jnp.concatenate([masked(2 * p), masked(2 * p + 1)], axis=1)
                x_blk = jnp.concatenate([jnp.where(low_half, xdt_p, 0.0), jnp.where(low_half, 0.0, xdt_p)],
                                        axis=0).astype(BF16)
                y_parts.append(jnp.dot(m2, x_blk, preferred_element_type=F32))
                yield
            y_g = (jnp.concatenate(y_parts, axis=1) + y_off * ecs_e[:, gcols] + dskip_ref[:, gcols] * xs_g)
            y_g = y_g * jax.nn.silu(z[:, gcols])
            y_ref[s, rows, gcols] = _rms(y_g, ng_ref[:, gcols]).astype(BF16)
            yield

    def chunk(c, carry):
        for _ in zip(*[chunk_stages(s, c) for s in range(SSD_SEQS)]):
            pass
        return carry

    lax.fori_loop(0, TS // L, chunk, 0)
    xpad[:, 0:HIST, :] = xpad[:, TS:TS + HIST, :]


def _ssd(xbc, z, dt, conv_w, conv_b, dt_bias_t, a_log_t, d_skip_exp, norm_g, batch, seq):
    t = xbc.shape[0]
    nblk = seq // TS
    consts = _ssd_constants()

    def tok(width):
        return pl.BlockSpec((SSD_SEQS, TS, width), lambda b, j: (b, j, 0))

    by_seq = lambda a: a.reshape(batch, seq, a.shape[-1])
    y = pl.pallas_call(
        _ssd_kernel,
        grid=(batch // SSD_SEQS, nblk),
        in_specs=[tok(SSD_CONV_CH), tok(SSD_INNER), tok(LANES)] + [_resident(m.shape) for m in consts] + [
            _resident((SSD_CONV, SSD_CONV_CH)), _resident((1, SSD_CONV_CH)),
            _resident((SSD_HEADS, SSD_CHUNK)), _resident((SSD_HEADS, SSD_CHUNK)),
            _resident((1, SSD_INNER)), _resident((1, SSD_INNER))],
        out_specs=tok(SSD_INNER),
        out_shape=jax.ShapeDtypeStruct((batch, seq, SSD_INNER), BF16),
        scratch_shapes=[pltpu.VMEM((SSD_SEQS, TS + HIST, SSD_CONV_CH), BF16),
                        pltpu.VMEM((SSD_SEQS, SSD_GROUPS, SSD_STATE, GROUP_W), F32)],
        compiler_params=_params("parallel", "arbitrary"),
        name="ssd",
    )(by_seq(xbc), by_seq(z), by_seq(dt), *consts, conv_w, conv_b, dt_bias_t, a_log_t, d_skip_exp, norm_g)
    return y.reshape(t, SSD_INNER)


def _rope_constants():
    e = np.zeros((LANES, 2 * HEAD_PAD), np.float32)
    for piece in range(3):
        for j in range(ROPE_HALF):
            c_row = piece * ROPE_HALF + j
            s_row = 3 * ROPE_HALF + piece * ROPE_HALF + j
            e[c_row, MLA_NOPE + j] = e[c_row, MLA_NOPE + ROPE_HALF + j] = 1.0
            e[s_row, MLA_QK + j] = e[s_row, MLA_QK + ROPE_HALF + j] = 1.0
            e[s_row, HEAD_PAD + MLA_NOPE + j] = -1.0
            e[s_row, HEAD_PAD + MLA_NOPE + ROPE_HALF + j] = 1.0
    return jnp.asarray(e, BF16)


def _mla_prep_kernel(qc_ref, kvc_ref, kr_ref, pos_ref, inv_ref, erope_ref, qg_ref, kvg_ref, wuq_ref, wukv_ref,
                     q_out, k_out, v_out):
    ang_t = inv_ref[...] * pos_ref[0].astype(F32)
    stack = jnp.concatenate(_bf16_pieces(jnp.cos(ang_t), 3) + _bf16_pieces(jnp.sin(ang_t), 3)
                            + [jnp.zeros((LANES - 6 * ROPE_HALF, TM), F32)], axis=0)
    tabs = jnp.dot(stack.T.astype(BF16), erope_ref[...], preferred_element_type=F32)
    q_raw = tabs[:, :HEAD_PAD]
    k_sin = tabs[:, HEAD_PAD:]
    lane = lax.broadcasted_iota(jnp.int32, q_raw.shape, 1)
    first = (lane >= MLA_NOPE) & (lane < MLA_NOPE + ROPE_HALF)
    q_tab = jnp.where(lane < MLA_NOPE, 1.0, q_raw) * (MLA_QK ** -0.5 * LOG2E)

    low_lanes = lane < MLA_NOPE
    kr = kr_ref[...]
    swapped = jnp.where(first, pltpu.roll(kr, HEAD_PAD - ROPE_HALF, 1), pltpu.roll(kr, ROPE_HALF, 1))
    roped = kr * q_raw + swapped * k_sin
    k_rope = roped + pltpu.roll(roped, MLA_ROPE, 1)

    qn = _rms(qc_ref[...].astype(F32), qg_ref[...]).astype(BF16)
    kvn = _rms(kvc_ref[...].astype(F32), kvg_ref[...]).astype(BF16)
    for hp in range(MLA_HEADS // 2):
        cols2 = slice(2 * hp * HEAD_PAD, (2 * hp + 2) * HEAD_PAD)
        q2 = jnp.dot(qn, wuq_ref[:, cols2], preferred_element_type=F32)
        kv2 = jnp.dot(kvn, wukv_ref[:, cols2], preferred_element_type=F32)
        for hh in range(2):
            cols = slice((2 * hp + hh) * HEAD_PAD, (2 * hp + hh + 1) * HEAD_PAD)
            half = slice(hh * HEAD_PAD, (hh + 1) * HEAD_PAD)
            q_out[:, cols] = (q2[:, half] * q_tab).astype(BF16)
            k_out[:, cols] = jnp.where(low_lanes, kv2[:, half], k_rope).astype(BF16)
            v_out[:, cols] = jnp.where(low_lanes, 1.0, kv2[:, half]).astype(BF16)


def _mla_prep(qc, kvc, kr, pos_rows, inv_col, q_g, kv_g, wuq_pad, wukv):
    t = qc.shape[0]
    e_rope = _rope_constants()

    def tok(width):
        return pl.BlockSpec((TM, width), lambda i: (i, 0))

    wide = MLA_HEADS * HEAD_PAD
    return pl.pallas_call(
        _mla_prep_kernel,
        grid=(t // TM,),
        in_specs=[tok(MLA_Q_RANK), tok(MLA_KV_RANK), tok(HEAD_PAD), pl.BlockSpec((1, 1, TM), lambda i: (i, 0, 0)),
                  _resident((ROPE_HALF, 1)), _resident(e_rope.shape),
                  _resident((1, MLA_Q_RANK)), _resident((1, MLA_KV_RANK)),
                  _resident((MLA_Q_RANK, wide)), _resident((MLA_KV_RANK, wide))],
        out_specs=[tok(wide)] * 3,
        out_shape=[jax.ShapeDtypeStruct((t, wide), BF16)] * 3,
        compiler_params=_params("parallel"),
        name="mla_prep",
    )(qc, kvc, kr, pos_rows, inv_col, e_rope, q_g, kv_g, wuq_pad, wukv)


def _attn_kernel(q_ref, k_ref, v_ref, o_ref):
    seq = q_ref.shape[0]
    causal = (lax.broadcasted_iota(jnp.int32, (TQ, TQ), 1) <= lax.broadcasted_iota(jnp.int32, (TQ, TQ), 0))
    low_half = lax.broadcasted_iota(jnp.int32, (TQ, HEAD_PAD), 1) < MLA_V
    nt = (((1,), (1,)), ((), ()))
    units = [(i, hh) for i in reversed(range(seq // TQ)) for hh in range(2)]

    def scores(i, hh):
        rows = slice(i * TQ, (i + 1) * TQ)
        hcols = slice(hh * HEAD_PAD, (hh + 1) * HEAD_PAD)
        q = q_ref[rows, hcols]
        s_diag = jnp.where(causal, lax.dot_general(q, k_ref[rows, hcols], nt, preferred_element_type=F32),
                           -jnp.inf)
        s_past = lax.dot_general(q, k_ref[0:i * TQ, hcols], nt, preferred_element_type=F32) if i else None
        return s_diag, s_past

    def weighted_values(i, hh, s_diag, s_past):
        rows = slice(i * TQ, (i + 1) * TQ)
        hcols = slice(hh * HEAD_PAD, (hh + 1) * HEAD_PAD)
        m = jnp.max(s_diag, axis=1, keepdims=True)
        if i:
            m = jnp.maximum(m, jnp.max(s_past, axis=1, keepdims=True))
        acc = jnp.dot(jnp.exp2(s_diag - m).astype(BF16), v_ref[rows, hcols], preferred_element_type=F32)
        if i:
            acc = acc + jnp.dot(jnp.exp2(s_past - m).astype(BF16), v_ref[0:i * TQ, hcols],
                                preferred_element_type=F32)
        return acc

    pending = [scores(*unit) for unit in units[:ATTN_LOOKAHEAD]]
    accs = []
    for n, (i, hh) in enumerate(units):
        current = pending.pop(0)
        if n + ATTN_LOOKAHEAD < len(units):
            pending.append(scores(*units[n + ATTN_LOOKAHEAD]))
        accs.append(weighted_values(i, hh, *current))
        if hh == 1:
            mixed = jnp.where(low_half, accs[1], accs[0])
            rolled = pltpu.roll(jnp.where(low_half, accs[0], accs[1]), MLA_V, 1)
            num = jnp.where(low_half, rolled, mixed)
            den = jnp.where(low_half, mixed, rolled)
            o_ref[i * TQ:(i + 1) * TQ, :] = (num / den).astype(BF16)
            accs = []


def _attention(qf, kf, vf, batch, seq):
    t = qf.shape[0]
    pair = pl.BlockSpec((seq, 2 * HEAD_PAD), lambda b, hp: (b, hp))
    return pl.pallas_call(
        _attn_kernel,
        grid=(batch, MLA_HEADS // 2),
        in_specs=[pair, pair, pair],
        out_specs=pl.BlockSpec((seq, 2 * MLA_V), lambda b, hp: (b, hp)),
        out_shape=jax.ShapeDtypeStruct((t, MLA_HEADS * MLA_V), BF16),
        compiler_params=_params("parallel", "parallel"),
        name="mla_attention",
    )(qf, kf, vf)


def _merge_kernel(x_ref, yn_ref, o_ref, gate_ref, gb_ref, wssd_ref, wmla_ref, wout_ref, post_g_ref, out_ref):
    mixed = []
    for r in _HALVES:
        y_ssd = jnp.dot(yn_ref[r, :], wssd_ref[...], preferred_element_type=F32)
        y_mla = jnp.dot(o_ref[r, :], wmla_ref[...], preferred_element_type=F32)
        gates = jax.nn.sigmoid(gate_ref[r, :].astype(F32) + gb_ref[...])
        mixed.append((gates[:, :D_MODEL] * y_ssd + gates[:, D_MODEL:] * y_mla).astype(BF16))
    hs = [jnp.dot(m, wout_ref[...], preferred_element_type=F32) for m in mixed]
    for r, h in zip(_HALVES, hs):
        out_ref[r, :] = x_ref[r, :] + _rms(h, post_g_ref[...])


def _merge(x, yn, o, gate, gate_bias, wssd, wmla, wout, post_g):
    t = x.shape[0]

    def tok(width):
        return pl.BlockSpec((TM, width), lambda i: (i, 0))

    sq = _resident((D_MODEL, D_MODEL))
    return pl.pallas_call(
        _merge_kernel,
        grid=(t // TM,),
        in_specs=[tok(D_MODEL), tok(SSD_INNER), tok(MLA_HEADS * MLA_V), tok(2 * D_MODEL),
                  _resident((1, 2 * D_MODEL)), sq, sq, sq, _resident((1, D_MODEL))],
        out_specs=tok(D_MODEL),
        out_shape=jax.ShapeDtypeStruct((t, D_MODEL), F32),
        compiler_params=_params("parallel"),
        name="merge",
    )(x, yn, o, gate, gate_bias, wssd, wmla, wout, post_g)


def _memkv_kernel(mem_ref, g_ref, wk_ref, wv_ref, k_ref, v_ref):
    mn = _rms(mem_ref[...], g_ref[...]).astype(BF16)
    k_ref[...] = jnp.dot(mn, wk_ref[...], preferred_element_type=F32).astype(BF16)
    v_ref[...] = jnp.dot(mn, wv_ref[...], preferred_element_type=F32).astype(BF16)


def _memkv(mem, g, wk, wv):
    t = mem.shape[0]
    rows = math.gcd(TM, t)
    blk = pl.BlockSpec((rows, D_MODEL), lambda b: (b, 0))
    sq = _resident((D_MODEL, D_MODEL))
    return pl.pallas_call(
        _memkv_kernel,
        grid=(t // rows,),
        in_specs=[blk, _resident((1, D_MODEL)), sq, sq],
        out_specs=[blk, blk],
        out_shape=[jax.ShapeDtypeStruct((t, D_MODEL), BF16)] * 2,
        compiler_params=_params("parallel"),
        name="memkv",
    )(mem, g, wk, wv)


def _xattn_kernel(x_ref, g_ref, wq_ref, k_ref, v_ref, wo_ref, post_g_ref, out_ref):
    heads = [slice(h * XA_HEAD_DIM, (h + 1) * XA_HEAD_DIM) for h in range(XA_HEADS)]
    nt = (((1,), (1,)), ((), ()))
    xs = [x_ref[r, :] for r in _HALVES]
    hns = [_rms(x, g_ref[...]).astype(BF16) for x in xs]
    qs = [(jnp.dot(hn, wq_ref[...], preferred_element_type=F32) * (XA_HEAD_DIM ** -0.5 * LOG2E)).astype(BF16)
          for hn in hns]
    scores = [[lax.dot_general(q[:, cols], k_ref[:, cols], nt, preferred_element_type=F32) for cols in heads]
              for q in qs]
    attended = []
    for head_scores in scores:
        outs = []
        for cols, s in zip(heads, head_scores):
            p = jnp.exp2(s - jnp.max(s, axis=1, keepdims=True))
            l = jnp.sum(p, axis=1, keepdims=True)
            o = jnp.dot(p.astype(BF16), v_ref[:, cols], preferred_element_type=F32) / l
            outs.append(o.astype(BF16))
        attended.append(jnp.concatenate(outs, axis=1))
    ys = [jnp.dot(o, wo_ref[...], preferred_element_type=F32) for o in attended]
    for r, x, y in zip(_HALVES, xs, ys):
        out_ref[r, :] = x + _rms(y, post_g_ref[...])


def _xattn(x, g, wq, k, v, wo, post_g, batch, seq, mem_len):
    t = x.shape[0]
    nblk = seq // TM
    tok = pl.BlockSpec((TM, D_MODEL), lambda b, j: (b * nblk + j, 0))
    kv = pl.BlockSpec((mem_len, D_MODEL), lambda b, j: (b, 0))
    sq = _resident((D_MODEL, D_MODEL))
    return pl.pallas_call(
        _xattn_kernel,
        grid=(batch, nblk),
        in_specs=[tok, _resident((1, D_MODEL)), sq, kv, kv, sq, _resident((1, D_MODEL))],
        out_specs=tok,
        out_shape=jax.ShapeDtypeStruct((t, D_MODEL), F32),
        compiler_params=_params("parallel", "arbitrary"),
        name="xattn",
    )(x, g, wq, k, v, wo, post_g)


def _row(v):
    return v.reshape(1, -1).astype(F32)


def _pad_cols(w, left, right):
    return jnp.pad(w, ((0, 0), (left, right)))


def _layer(x, mem, pos, l, p, batch, seq, mem_len):
    bf = lambda w: w.astype(BF16)

    x = _ffn(x, _row(p["ffn1_pre_g"][l]), bf(p["ffn1_w_gate"][l]), bf(p["ffn1_w_up"][l]),
             bf(p["ffn1_w_down"][l]), _row(p["ffn1_post_g"][l]))

    w_in = p["w_in"][l]
    o_z = 0
    o_xbc = o_z + SSD_INNER
    o_dt = o_xbc + SSD_CONV_CH
    o_qc = o_dt + SSD_HEADS
    o_kvc = o_qc + MLA_Q_RANK
    o_kr = o_kvc + MLA_KV_RANK
    o_gate = o_kr + MLA_ROPE
    w_arr = jnp.concatenate([
        w_in[:, o_z:o_xbc], w_in[:, o_xbc:o_dt], w_in[:, o_gate:], w_in[:, o_qc:o_kvc], w_in[:, o_kvc:o_kr],
        w_in[:, o_dt:o_qc], jnp.zeros((D_MODEL, MLA_NOPE - SSD_HEADS), w_in.dtype),
        _pad_cols(w_in[:, o_kr:o_gate], 0, HEAD_PAD - MLA_QK)], axis=1)
    z, xbc, gate, qc, kvc, dtkr = _inproj(x, _row(p["mix_pre_g"][l]), bf(w_arr))

    d_skip_exp = jnp.repeat(p["d_skip"][l].astype(F32), SSD_HEAD_DIM).reshape(1, SSD_INNER)
    per_head_rows = lambda v: jnp.broadcast_to(v.astype(F32)[:, None], (SSD_HEADS, SSD_CHUNK))
    yn = _ssd(xbc, z, dtkr, p["conv_w"][l].astype(F32), _row(p["conv_b"][l]), per_head_rows(p["dt_bias"][l]),
              per_head_rows(p["a_log"][l]), d_skip_exp, _row(p["ssd_norm_g"][l]), batch, seq)

    inv = ROPE_THETA ** (-jnp.arange(0, MLA_ROPE, 2, dtype=F32) / MLA_ROPE)
    inv_col = inv.reshape(ROPE_HALF, 1)
    wq = p["w_uq"][l].reshape(MLA_Q_RANK, MLA_HEADS, MLA_QK)
    q_x1 = wq[..., MLA_NOPE:MLA_NOPE + ROPE_HALF]
    q_x2 = wq[..., MLA_NOPE + ROPE_HALF:]
    wuq_pad = jnp.concatenate([wq, -q_x2, q_x1], axis=-1).reshape(MLA_Q_RANK, MLA_HEADS * HEAD_PAD)
    wukv = jnp.concatenate([p["w_uk"][l].reshape(MLA_KV_RANK, MLA_HEADS, MLA_NOPE),
                            p["w_uv"][l].reshape(MLA_KV_RANK, MLA_HEADS, MLA_V)],
                           axis=-1).reshape(MLA_KV_RANK, MLA_HEADS * HEAD_PAD)
    qf, kf, vf = _mla_prep(qc, kvc, dtkr, pos, inv_col, _row(p["q_norm_g"][l]), _row(p["kv_norm_g"][l]),
                           bf(wuq_pad), bf(wukv))
    o = _attention(qf, kf, vf, batch, seq)

    w_mla = p["w_mla_proj"][l].reshape(MLA_HEADS // 2, 2, MLA_V, D_MODEL)[:, ::-1].reshape(MLA_HEADS * MLA_V, D_MODEL)
    x = _merge(x, yn, o, gate, _row(p["gate_bias"][l]), bf(p["w_ssd_proj"][l]), bf(w_mla),
               bf(p["w_out"][l]), _row(p["mix_post_g"][l]))

    mk, mv = _memkv(mem, _row(p["mem_norm_g"][l]), bf(p["w_xk"][l]), bf(p["w_xv"][l]))
    x = _xattn(x, _row(p["xa_pre_g"][l]), bf(p["w_xq"][l]), mk, mv, bf(p["w_xo"][l]), _row(p["xa_post_g"][l]),
               batch, seq, mem_len)

    return _ffn(x, _row(p["ffn2_pre_g"][l]), bf(p["ffn2_w_gate"][l]), bf(p["ffn2_w_up"][l]),
                bf(p["ffn2_w_down"][l]), _row(p["ffn2_post_g"][l]))


def kernel(x, mem, positions, ffn1_pre_g, ffn1_w_gate, ffn1_w_up, ffn1_w_down, ffn1_post_g, mix_pre_g, w_in, conv_w, conv_b, dt_bias, a_log, d_skip, ssd_norm_g, w_ssd_proj, q_norm_g, w_uq, kv_norm_g, w_uk, w_uv, w_mla_proj, gate_bias, w_out, mix_post_g, xa_pre_g, mem_norm_g, w_xq, w_xk, w_xv, w_xo, xa_post_g, ffn2_pre_g, ffn2_w_gate, ffn2_w_up, ffn2_w_down, ffn2_post_g):
    p = dict(ffn1_pre_g=ffn1_pre_g, ffn1_w_gate=ffn1_w_gate, ffn1_w_up=ffn1_w_up, ffn1_w_down=ffn1_w_down,
             ffn1_post_g=ffn1_post_g, mix_pre_g=mix_pre_g, w_in=w_in, conv_w=conv_w, conv_b=conv_b,
             dt_bias=dt_bias, a_log=a_log, d_skip=d_skip, ssd_norm_g=ssd_norm_g, w_ssd_proj=w_ssd_proj,
             q_norm_g=q_norm_g, w_uq=w_uq, kv_norm_g=kv_norm_g, w_uk=w_uk, w_uv=w_uv, w_mla_proj=w_mla_proj,
             gate_bias=gate_bias, w_out=w_out, mix_post_g=mix_post_g, xa_pre_g=xa_pre_g, mem_norm_g=mem_norm_g,
             w_xq=w_xq, w_xk=w_xk, w_xv=w_xv, w_xo=w_xo, xa_post_g=xa_post_g, ffn2_pre_g=ffn2_pre_g,
             ffn2_w_gate=ffn2_w_gate, ffn2_w_up=ffn2_w_up, ffn2_w_down=ffn2_w_down, ffn2_post_g=ffn2_post_g)
    batch, seq, _ = x.shape
    mem_len = mem.shape[1]
    assert seq % TS == 0 and seq % TM == 0 and seq % TQ == 0 and TS % SSD_CHUNK == 0 and batch % SSD_SEQS == 0
    xf = x.reshape(batch * seq, D_MODEL)
    memf = mem.reshape(batch * mem_len, D_MODEL)
    pos = positions.reshape(batch * seq // TM, 1, TM)
    for l in range(w_in.shape[0]):
        xf = _layer(xf, memf, pos, l, p, batch, seq, mem_len)
    return xf.reshape(batch, seq, D_MODEL)
```

```python
import math

import numpy as np
import jax
import jax.numpy as jnp
from jax import lax
from jax.experimental import pallas as pl
from jax.experimental.pallas import tpu as pltpu

F32 = jnp.float32
BF16 = jnp.bfloat16

D_MODEL = 1024
SSD_HEADS = 16
SSD_HEAD_DIM = 64
SSD_INNER = SSD_HEADS * SSD_HEAD_DIM
SSD_GROUPS = 2
SSD_STATE = 128
SSD_CONV = 4
SSD_CHUNK = 128
SSD_CONV_CH = SSD_INNER + 2 * SSD_GROUPS * SSD_STATE
MLA_HEADS = 16
MLA_Q_RANK = 384
MLA_KV_RANK = 256
MLA_NOPE = 64
MLA_ROPE = 32
MLA_V = 64
MLA_QK = MLA_NOPE + MLA_ROPE
ROPE_THETA = 10000.0
XA_HEADS = 4
XA_HEAD_DIM = D_MODEL // XA_HEADS
D_FF = 2816
FFN_RES_WEIGHT = 0.5
EPS = 1e-6

LANES = 128
BF16_ROWS = 16
HEAD_PAD = LANES
ROPE_HALF = MLA_ROPE // 2
GROUP_W = SSD_INNER // SSD_GROUPS
PAIRS_PER_GROUP = GROUP_W // LANES
LOG2E = math.log2(math.e)

TM = 1024
FFN_CHUNK = 256
TS = 512
SSD_SEQS = 4
TQ = 256
ATTN_LOOKAHEAD = 3
HIST = BF16_ROWS
VMEM_LIMIT = 56 * 1024 * 1024

_C_Z = (0, SSD_INNER)
_C_XBC = (_C_Z[1], _C_Z[1] + SSD_CONV_CH)
W_HEAD = _C_XBC[1]
_C_QC = (0, MLA_Q_RANK)
_C_KVC = (_C_QC[1], _C_QC[1] + MLA_KV_RANK)
_C_DTKR = (_C_KVC[1], _C_KVC[1] + LANES)
W_SMALL = _C_DTKR[1]

_ROWS_DT, _ROWS_ECS, _ROWS_DSF = (0, 32), (32, 64), (64, 96)
_ROWS_CS = (0, 48)

_HALVES = (slice(0, TM // 2), slice(TM // 2, TM))


def _rms(x, g):
    return x * lax.rsqrt(jnp.mean(x * x, axis=-1, keepdims=True) + EPS) * g


def _softplus(x):
    return jnp.maximum(x, 0.0) + jnp.log1p(jnp.exp(-jnp.abs(x)))


def _bf16_pieces(x, n):
    pieces, rest = [], x
    for _ in range(n):
        piece = rest.astype(BF16).astype(F32)
        pieces.append(piece)
        rest = rest - piece
    return pieces


def _resident(shape):
    zeros = (0,) * len(shape)
    return pl.BlockSpec(shape, lambda *_: zeros, pipeline_mode=pl.Buffered(1))


def _params(*sem):
    return pltpu.CompilerParams(dimension_semantics=sem, vmem_limit_bytes=VMEM_LIMIT)


def _ffn_kernel(x_ref, pre_g_ref, wg_ref, wu_ref, wd_ref, post_g_ref, o_ref, h_ref):
    xs = [x_ref[r, :] for r in _HALVES]
    xns = [_rms(x, pre_g_ref[...]).astype(BF16) for x in xs]
    for c in range(D_FF // FFN_CHUNK):
        sl = slice(c * FFN_CHUNK, (c + 1) * FFN_CHUNK)
        for r, xn in zip(_HALVES, xns):
            g = jnp.dot(xn, wg_ref[:, sl], preferred_element_type=F32)
            u = jnp.dot(xn, wu_ref[:, sl], preferred_element_type=F32)
            h_ref[r, sl] = (jax.nn.silu(g) * u).astype(BF16)
    ys = [jnp.dot(h_ref[r, :], wd_ref[...], preferred_element_type=F32) for r in _HALVES]
    for r, x, y in zip(_HALVES, xs, ys):
        o_ref[r, :] = x + FFN_RES_WEIGHT * _rms(y, post_g_ref[...])


def _ffn(x, pre_g, wg, wu, wd, post_g):
    t = x.shape[0]
    tok = pl.BlockSpec((TM, D_MODEL), lambda i: (i, 0))
    return pl.pallas_call(
        _ffn_kernel,
        grid=(t // TM,),
        in_specs=[tok, _resident((1, D_MODEL)), _resident((D_MODEL, D_FF)), _resident((D_MODEL, D_FF)),
                  _resident((D_FF, D_MODEL)), _resident((1, D_MODEL))],
        out_specs=tok,
        out_shape=jax.ShapeDtypeStruct((t, D_MODEL), F32),
        scratch_shapes=[pltpu.VMEM((TM, D_FF), BF16)],
        compiler_params=_params("parallel"),
        name="ffn",
    )(x, pre_g, wg, wu, wd, post_g)


def _inproj_kernel(x_ref, g_ref, whead_ref, wgate_ref, wsmall_ref, z_ref, xbc_ref, gate_ref, qc_ref, kvc_ref,
                   dtkr_ref):
    hns = [_rms(x_ref[r, :], g_ref[...]).astype(BF16) for r in _HALVES]
    outs = ((z_ref, whead_ref, _C_Z), (xbc_ref, whead_ref, _C_XBC), (gate_ref, wgate_ref, (0, 2 * D_MODEL)),
            (qc_ref, wsmall_ref, _C_QC), (kvc_ref, wsmall_ref, _C_KVC), (dtkr_ref, wsmall_ref, _C_DTKR))
    for out_ref, w_ref, cols in outs:
        for r, hn in zip(_HALVES, hns):
            out_ref[r, :] = jnp.dot(hn, w_ref[:, cols[0]:cols[1]],
                                    preferred_element_type=F32).astype(out_ref.dtype)


def _inproj(x, g, w_in_bf, w_gate, w_small):
    t = x.shape[0]

    def tok(width):
        return pl.BlockSpec((TM, width), lambda i: (i, 0))

    widths = (SSD_INNER, SSD_CONV_CH, 2 * D_MODEL, MLA_Q_RANK, MLA_KV_RANK, LANES)
    dtypes = (BF16, BF16, BF16, BF16, BF16, F32)
    return pl.pallas_call(
        _inproj_kernel,
        grid=(t // TM,),
        in_specs=[tok(D_MODEL), _resident((1, D_MODEL)), _resident((D_MODEL, W_HEAD)),
                  _resident((D_MODEL, 2 * D_MODEL)), _resident((D_MODEL, W_SMALL))],
        out_specs=[tok(w) for w in widths],
        out_shape=[jax.ShapeDtypeStruct((t, w), d) for w, d in zip(widths, dtypes)],
        compiler_params=_params("parallel"),
        name="inproj",
    )(x, g, w_in_bf, w_gate, w_small)


def _ssd_constants():
    L = SSD_CHUNK
    shift = np.zeros(((SSD_CONV - 1) * L, L + HIST), np.float32)
    for k in range(SSD_CONV - 1):
        shift[k * L + np.arange(L), np.arange(L) + HIST - (SSD_CONV - 1) + k] = 1.0
    head_of_lane = np.arange(SSD_INNER) // SSD_HEAD_DIM
    exp_ch = np.zeros((LANES, 3 * SSD_INNER), np.float32)
    for q, (lo, hi) in enumerate((_ROWS_DT, _ROWS_ECS, _ROWS_DSF)):
        for r in range(lo, hi):
            exp_ch[r, q * SSD_INNER + np.nonzero(head_of_lane == r % SSD_HEADS)[0]] = 1.0
    exp_blk = np.zeros((LANES, SSD_HEADS * LANES), np.float32)
    for r in range(*_ROWS_CS):
        h = r % SSD_HEADS
        exp_blk[r, h * LANES:(h + 1) * LANES] = 1.0
    tri = np.tile((np.arange(L)[:, None] <= np.arange(L)[None, :]).astype(np.float32), (3, 1))
    return tuple(jnp.asarray(m, BF16) for m in (shift, exp_ch, exp_blk, tri))


def _ssd_kernel(xbc_ref, z_ref, dt_ref, shift_ref, expch_ref, expblk_ref, tri_ref, cw_ref, cb_ref, dtb_ref,
                alog_ref, dskip_ref, ng_ref, y_ref, xpad, state):
    L = SSD_CHUNK

    @pl.when(pl.program_id(1) == 0)
    def _start_of_sequence():
        xpad[:, 0:HIST, :] = jnp.zeros((SSD_SEQS, HIST, SSD_CONV_CH), BF16)
        state[...] = jnp.zeros_like(state)

    xpad[:, HIST:HIST + TS, :] = xbc_ref[...]

    low_half = lax.broadcasted_iota(jnp.int32, (L, LANES), 1) < SSD_HEAD_DIM
    tril = lax.broadcasted_iota(jnp.int32, (L, L), 1) <= lax.broadcasted_iota(jnp.int32, (L, L), 0)
    a_neg = -jnp.exp(alog_ref[...])

    def to_token_rows(stack):
        full = jnp.concatenate([stack, jnp.zeros((LANES - stack.shape[0], L), F32)], axis=0)
        return full.T.astype(BF16)

    def chunk_stages(s, c):
        r0 = pl.multiple_of(c * L, L)
        rows = pl.ds(r0, L)
        win = xpad[s, pl.ds(r0, L + HIST), :]
        taps = jnp.dot(shift_ref[...], win, preferred_element_type=F32)
        yield
        dt_t = _softplus(dt_ref[s, rows, :].T[:SSD_HEADS, :] + dtb_ref[...])
        adt_t = dt_t * a_neg
        cs_t = jnp.dot(jnp.concatenate(_bf16_pieces(adt_t, 3), axis=1).astype(BF16), tri_ref[...],
                       preferred_element_type=F32)
        yield
        ecs_t = jnp.exp(cs_t)
        dsf_t = jnp.exp(cs_t[:, L - 1:L] - cs_t)
        stack = jnp.concatenate(_bf16_pieces(dt_t, 2) + _bf16_pieces(ecs_t, 2) + _bf16_pieces(dsf_t, 2), axis=0)
        per_ch = jnp.dot(to_token_rows(stack), expch_ref[...], preferred_element_type=F32)
        dt_e = per_ch[:, :SSD_INNER]
        ecs_e = per_ch[:, SSD_INNER:2 * SSD_INNER]
        dsf_e = per_ch[:, 2 * SSD_INNER:]
        cs_col = jnp.dot(to_token_rows(jnp.concatenate(_bf16_pieces(cs_t, 3), axis=0)), expblk_ref[...],
                         preferred_element_type=F32)
        yield
        u = cb_ref[...] + cw_ref[SSD_CONV - 1:SSD_CONV, :] * win[HIST:, :].astype(F32)
        for k in range(SSD_CONV - 1):
            u = u + cw_ref[k:k + 1, :] * taps[k * L:(k + 1) * L, :]
        u = jax.nn.silu(u)
        xs = u[:, :SSD_INNER]
        bmat = u[:, SSD_INNER:SSD_INNER + SSD_GROUPS * SSD_STATE]
        cmat = u[:, SSD_INNER + SSD_GROUPS * SSD_STATE:]
        z = z_ref[s, rows, :].astype(F32)
        yield

        for g in range(SSD_GROUPS):
            gcols = slice(g * GROUP_W, (g + 1) * GROUP_W)
            bm_g = bmat[:, g * SSD_STATE:(g + 1) * SSD_STATE]
            cm_g = cmat[:, g * SSD_STATE:(g + 1) * SSD_STATE].astype(BF16)
            cbm = lax.dot_general(cm_g, bm_g.astype(BF16), (((1,), (1,)), ((), ())),
                                  preferred_element_type=F32)
            s_g = state[s, g]
            y_off = jnp.dot(cm_g, s_g.astype(BF16), preferred_element_type=F32)
            xs_g = xs[:, gcols]
            xdt_g = xs_g * dt_e[:, gcols]
            xw_g = (xdt_g * dsf_e[:, gcols]).astype(BF16)
            state[s, g] = (s_g * ecs_e[L - 1:L, gcols]
                           + jnp.dot(bm_g.T.astype(BF16), xw_g, preferred_element_type=F32))
            yield
            y_parts = []
            for pp in range(PAIRS_PER_GROUP):
                p = g * PAIRS_PER_GROUP + pp

                def masked(h):
                    dec = jnp.exp(jnp.where(tril, cs_col[:, h * LANES:(h + 1) * LANES] - cs_t[h:h + 1, :],
                                            -jnp.inf))
                    return (cbm * dec).astype(BF16)

                xdt_p = xdt_g[:, pp * LANES:(pp + 1) * LANES]
                m2 = jnp.concatenate([masked(2 * p), masked(2 * p + 1)], axis=1)
                x_blk = jnp.concatenate([jnp.where(low_half, xdt_p, 0.0), jnp.where(low_half, 0.0, xdt_p)],
                                        axis=0).astype(BF16)
                y_parts.append(jnp.dot(m2, x_blk, preferred_element_type=F32))
                yield
            y_g = (jnp.concatenate(y_parts, axis=1) + y_off * ecs_e[:, gcols] + dskip_ref[:, gcols] * xs_g)
            y_g = y_g * jax.nn.silu(z[:, gcols])
            y_ref[s, rows, gcols] = _rms(y_g, ng_ref[:, gcols]).astype(BF16)
            yield

    def chunk(c, carry):
        for _ in zip(*[chunk_stages(s, c) for s in range(SSD_SEQS)]):
            pass
        return carry

    lax.fori_loop(0, TS // L, chunk, 0)
    xpad[:, 0:HIST, :] = xpad[:, TS:TS + HIST, :]


def _ssd(xbc, z, dt, conv_w, conv_b, dt_bias_t, a_log_t, d_skip_exp, norm_g, batch, seq):
    t = xbc.shape[0]
    nblk = seq // TS
    consts = _ssd_constants()

    def tok(width):
        return pl.BlockSpec((SSD_SEQS, TS, width), lambda b, j: (b, j, 0))

    by_seq = lambda a: a.reshape(batch, seq, a.shape[-1])
    y = pl.pallas_call(
        _ssd_kernel,
        grid=(batch // SSD_SEQS, nblk),
        in_specs=[tok(SSD_CONV_CH), tok(SSD_INNER), tok(LANES)] + [_resident(m.shape) for m in consts] + [
            _resident((SSD_CONV, SSD_CONV_CH)), _resident((1, SSD_CONV_CH)),
            _resident((SSD_HEADS, SSD_CHUNK)), _resident((SSD_HEADS, SSD_CHUNK)),
            _resident((1, SSD_INNER)), _resident((1, SSD_INNER))],
        out_specs=tok(SSD_INNER),
        out_shape=jax.ShapeDtypeStruct((batch, seq, SSD_INNER), BF16),
        scratch_shapes=[pltpu.VMEM((SSD_SEQS, TS + HIST, SSD_CONV_CH), BF16),
                        pltpu.VMEM((SSD_SEQS, SSD_GROUPS, SSD_STATE, GROUP_W), F32)],
        compiler_params=_params("parallel", "arbitrary"),
        name="ssd",
    )(by_seq(xbc), by_seq(z), by_seq(dt), *consts, conv_w, conv_b, dt_bias_t, a_log_t, d_skip_exp, norm_g)
    return y.reshape(t, SSD_INNER)


def _rope_constants():
    e = np.zeros((LANES, 2 * HEAD_PAD), np.float32)
    for piece in range(3):
        for j in range(ROPE_HALF):
            c_row = piece * ROPE_HALF + j
            s_row = 3 * ROPE_HALF + piece * ROPE_HALF + j
            e[c_row, MLA_NOPE + j] = e[c_row, MLA_NOPE + ROPE_HALF + j] = 1.0
            e[s_row, MLA_QK + j] = e[s_row, MLA_QK + ROPE_HALF + j] = 1.0
            e[s_row, HEAD_PAD + MLA_NOPE + j] = -1.0
            e[s_row, HEAD_PAD + MLA_NOPE + ROPE_HALF + j] = 1.0
    return jnp.asarray(e, BF16)


def _mla_prep_kernel(qc_ref, kvc_ref, kr_ref, pos_ref, inv_ref, erope_ref, qg_ref, kvg_ref, wuq_ref, wukv_ref,
                     q_out, k_out, v_out):
    ang_t = inv_ref[...] * pos_ref[0].astype(F32)
    stack = jnp.concatenate(_bf16_pieces(jnp.cos(ang_t), 3) + _bf16_pieces(jnp.sin(ang_t), 3)
                            + [jnp.zeros((LANES - 6 * ROPE_HALF, TM), F32)], axis=0)
    tabs = jnp.dot(stack.T.astype(BF16), erope_ref[...], preferred_element_type=F32)
    q_raw = tabs[:, :HEAD_PAD]
    k_sin = tabs[:, HEAD_PAD:]
    lane = lax.broadcasted_iota(jnp.int32, q_raw.shape, 1)
    first = (lane >= MLA_NOPE) & (lane < MLA_NOPE + ROPE_HALF)
    q_tab = jnp.where(lane < MLA_NOPE, 1.0, q_raw) * (MLA_QK ** -0.5 * LOG2E)

    low_lanes = lane < MLA_NOPE
    kr = kr_ref[...]
    swapped = jnp.where(first, pltpu.roll(kr, HEAD_PAD - ROPE_HALF, 1), pltpu.roll(kr, ROPE_HALF, 1))
    roped = kr * q_raw + swapped * k_sin
    k_rope = roped + pltpu.roll(roped, MLA_ROPE, 1)

    qn = _rms(qc_ref[...].astype(F32), qg_ref[...]).astype(BF16)
    kvn = _rms(kvc_ref[...].astype(F32), kvg_ref[...]).astype(BF16)
    for hp in range(MLA_HEADS // 2):
        cols2 = slice(2 * hp * HEAD_PAD, (2 * hp + 2) * HEAD_PAD)
        q2 = jnp.dot(qn, wuq_ref[:, cols2], preferred_element_type=F32)
        kv2 = jnp.dot(kvn, wukv_ref[:, cols2], preferred_element_type=F32)
        for hh in range(2):
            cols = slice((2 * hp + hh) * HEAD_PAD, (2 * hp + hh + 1) * HEAD_PAD)
            half = slice(hh * HEAD_PAD, (hh + 1) * HEAD_PAD)
            q_out[:, cols] = (q2[:, half] * q_tab).astype(BF16)
            k_out[:, cols] = jnp.where(low_lanes, kv2[:, half], k_rope).astype(BF16)
            v_out[:, cols] = jnp.where(low_lanes, 1.0, kv2[:, half]).astype(BF16)


def _mla_prep(qc, kvc, kr, pos_rows, inv_col, q_g, kv_g, wuq_pad, wukv):
    t = qc.shape[0]
    e_rope = _rope_constants()

    def tok(width):
        return pl.BlockSpec((TM, width), lambda i: (i, 0))

    wide = MLA_HEADS * HEAD_PAD
    return pl.pallas_call(
        _mla_prep_kernel,
        grid=(t // TM,),
        in_specs=[tok(MLA_Q_RANK), tok(MLA_KV_RANK), tok(HEAD_PAD), pl.BlockSpec((1, 1, TM), lambda i: (i, 0, 0)),
                  _resident((ROPE_HALF, 1)), _resident(e_rope.shape),
                  _resident((1, MLA_Q_RANK)), _resident((1, MLA_KV_RANK)),
                  _resident((MLA_Q_RANK, wide)), _resident((MLA_KV_RANK, wide))],
        out_specs=[tok(wide)] * 3,
        out_shape=[jax.ShapeDtypeStruct((t, wide), BF16)] * 3,
        compiler_params=_params("parallel"),
        name="mla_prep",
    )(qc, kvc, kr, pos_rows, inv_col, e_rope, q_g, kv_g, wuq_pad, wukv)


def _attn_kernel(q_ref, k_ref, v_ref, o_ref):
    seq = q_ref.shape[0]
    causal = (lax.broadcasted_iota(jnp.int32, (TQ, TQ), 1) <= lax.broadcasted_iota(jnp.int32, (TQ, TQ), 0))
    low_half = lax.broadcasted_iota(jnp.int32, (TQ, HEAD_PAD), 1) < MLA_V
    nt = (((1,), (1,)), ((), ()))
    units = [(i, hh) for i in reversed(range(seq // TQ)) for hh in range(2)]

    def scores(i, hh):
        rows = slice(i * TQ, (i + 1) * TQ)
        hcols = slice(hh * HEAD_PAD, (hh + 1) * HEAD_PAD)
        q = q_ref[rows, hcols]
        s_diag = jnp.where(causal, lax.dot_general(q, k_ref[rows, hcols], nt, preferred_element_type=F32),
                           -jnp.inf)
        s_past = lax.dot_general(q, k_ref[0:i * TQ, hcols], nt, preferred_element_type=F32) if i else None
        return s_diag, s_past

    def weighted_values(i, hh, s_diag, s_past):
        rows = slice(i * TQ, (i + 1) * TQ)
        hcols = slice(hh * HEAD_PAD, (hh + 1) * HEAD_PAD)
        m = jnp.max(s_diag, axis=1, keepdims=True)
        if i:
            m = jnp.maximum(m, jnp.max(s_past, axis=1, keepdims=True))
        acc = jnp.dot(jnp.exp2(s_diag - m).astype(BF16), v_ref[rows, hcols], preferred_element_type=F32)
        if i:
            acc = acc + jnp.dot(jnp.exp2(s_past - m).astype(BF16), v_ref[0:i * TQ, hcols],
                                preferred_element_type=F32)
        return acc

    pending = [scores(*unit) for unit in units[:ATTN_LOOKAHEAD]]
    accs = []
    for n, (i, hh) in enumerate(units):
        current = pending.pop(0)
        if n + ATTN_LOOKAHEAD < len(units):
            pending.append(scores(*units[n + ATTN_LOOKAHEAD]))
        accs.append(weighted_values(i, hh, *current))
        if hh == 1:
            mixed = jnp.where(low_half, accs[1], accs[0])
            rolled = pltpu.roll(jnp.where(low_half, accs[0], accs[1]), MLA_V, 1)
            num = jnp.where(low_half, rolled, mixed)
            den = jnp.where(low_half, mixed, rolled)
            o_ref[i * TQ:(i + 1) * TQ, :] = (num / den).astype(BF16)
            accs = []


def _attention(qf, kf, vf, batch, seq):
    t = qf.shape[0]
    pair = pl.BlockSpec((seq, 2 * HEAD_PAD), lambda b, hp: (b, hp))
    return pl.pallas_call(
        _attn_kernel,
        grid=(batch, MLA_HEADS // 2),
        in_specs=[pair, pair, pair],
        out_specs=pl.BlockSpec((seq, 2 * MLA_V), lambda b, hp: (b, hp)),
        out_shape=jax.ShapeDtypeStruct((t, MLA_HEADS * MLA_V), BF16),
        compiler_params=_params("parallel", "parallel"),
        name="mla_attention",
    )(qf, kf, vf)


def _merge_kernel(x_ref, yn_ref, o_ref, gate_ref, gb_ref, wssd_ref, wmla_ref, wout_ref, post_g_ref, out_ref):
    mixed = []
    for r in _HALVES:
        y_ssd = jnp.dot(yn_ref[r, :], wssd_ref[...], preferred_element_type=F32)
        y_mla = jnp.dot(o_ref[r, :], wmla_ref[...], preferred_element_type=F32)
        gates = jax.nn.sigmoid(gate_ref[r, :].astype(F32) + gb_ref[...])
        mixed.append((gates[:, :D_MODEL] * y_ssd + gates[:, D_MODEL:] * y_mla).astype(BF16))
    hs = [jnp.dot(m, wout_ref[...], preferred_element_type=F32) for m in mixed]
    for r, h in zip(_HALVES, hs):
        out_ref[r, :] = x_ref[r, :] + _rms(h, post_g_ref[...])


def _merge(x, yn, o, gate, gate_bias, wssd, wmla, wout, post_g):
    t = x.shape[0]

    def tok(width):
        return pl.BlockSpec((TM, width), lambda i: (i, 0))

    sq = _resident((D_MODEL, D_MODEL))
    return pl.pallas_call(
        _merge_kernel,
        grid=(t // TM,),
        in_specs=[tok(D_MODEL), tok(SSD_INNER), tok(MLA_HEADS * MLA_V), tok(2 * D_MODEL),
                  _resident((1, 2 * D_MODEL)), sq, sq, sq, _resident((1, D_MODEL))],
        out_specs=tok(D_MODEL),
        out_shape=jax.ShapeDtypeStruct((t, D_MODEL), F32),
        compiler_params=_params("parallel"),
        name="merge",
    )(x, yn, o, gate, gate_bias, wssd, wmla, wout, post_g)


def _memkv_kernel(mem_ref, g_ref, wk_ref, wv_ref, k_ref, v_ref):
    mn = _rms(mem_ref[...], g_ref[...]).astype(BF16)
    k_ref[...] = jnp.dot(mn, wk_ref[...], preferred_element_type=F32).astype(BF16)
    v_ref[...] = jnp.dot(mn, wv_ref[...], preferred_element_type=F32).astype(BF16)


def _memkv(mem, g, wk, wv):
    t = mem.shape[0]
    rows = math.gcd(TM, t)
    blk = pl.BlockSpec((rows, D_MODEL), lambda b: (b, 0))
    sq = _resident((D_MODEL, D_MODEL))
    return pl.pallas_call(
        _memkv_kernel,
        grid=(t // rows,),
        in_specs=[blk, _resident((1, D_MODEL)), sq, sq],
        out_specs=[blk, blk],
        out_shape=[jax.ShapeDtypeStruct((t, D_MODEL), BF16)] * 2,
        compiler_params=_params("parallel"),
        name="memkv",
    )(mem, g, wk, wv)


def _xattn_kernel(x_ref, g_ref, wq_ref, k_ref, v_ref, wo_ref, post_g_ref, out_ref):
    heads = [slice(h * XA_HEAD_DIM, (h + 1) * XA_HEAD_DIM) for h in range(XA_HEADS)]
    nt = (((1,), (1,)), ((), ()))
    xs = [x_ref[r, :] for r in _HALVES]
    hns = [_rms(x, g_ref[...]).astype(BF16) for x in xs]
    qs = [(jnp.dot(hn, wq_ref[...], preferred_element_type=F32) * (XA_HEAD_DIM ** -0.5 * LOG2E)).astype(BF16)
          for hn in hns]
    scores = [[lax.dot_general(q[:, cols], k_ref[:, cols], nt, preferred_element_type=F32) for cols in heads]
              for q in qs]
    attended = []
    for head_scores in scores:
        outs = []
        for cols, s in zip(heads, head_scores):
            p = jnp.exp2(s - jnp.max(s, axis=1, keepdims=True))
            l = jnp.sum(p, axis=1, keepdims=True)
            o = jnp.dot(p.astype(BF16), v_ref[:, cols], preferred_element_type=F32) / l
            outs.append(o.astype(BF16))
        attended.append(jnp.concatenate(outs, axis=1))
    ys = [jnp.dot(o, wo_ref[...], preferred_element_type=F32) for o in attended]
    for r, x, y in zip(_HALVES, xs, ys):
        out_ref[r, :] = x + _rms(y, post_g_ref[...])


def _xattn(x, g, wq, k, v, wo, post_g, batch, seq, mem_len):
    t = x.shape[0]
    nblk = seq // TM
    tok = pl.BlockSpec((TM, D_MODEL), lambda b, j: (b * nblk + j, 0))
    kv = pl.BlockSpec((mem_len, D_MODEL), lambda b, j: (b, 0))
    sq = _resident((D_MODEL, D_MODEL))
    return pl.pallas_call(
        _xattn_kernel,
        grid=(batch, nblk),
        in_specs=[tok, _resident((1, D_MODEL)), sq, kv, kv, sq, _resident((1, D_MODEL))],
        out_specs=tok,
        out_shape=jax.ShapeDtypeStruct((t, D_MODEL), F32),
        compiler_params=_params("parallel", "arbitrary"),
        name="xattn",
    )(x, g, wq, k, v, wo, post_g)


def _row(v):
    return v.reshape(1, -1).astype(F32)


def _pad_cols(w, left, right):
    return jnp.pad(w, ((0, 0), (left, right)))


def _layer(x, mem, pos, l, p, batch, seq, mem_len):
    bf = lambda w: w.astype(BF16)

    x = _ffn(x, _row(p["ffn1_pre_g"][l]), bf(p["ffn1_w_gate"][l]), bf(p["ffn1_w_up"][l]),
             bf(p["ffn1_w_down"][l]), _row(p["ffn1_post_g"][l]))

    w_in = bf(p["w_in"][l])
    o_dt = SSD_INNER + SSD_CONV_CH
    o_qc = o_dt + SSD_HEADS
    o_kvc = o_qc + MLA_Q_RANK
    o_kr = o_kvc + MLA_KV_RANK
    o_gate = o_kr + MLA_ROPE
    w_small = jnp.concatenate([
        w_in[:, o_qc:o_kr], w_in[:, o_dt:o_qc], jnp.zeros((D_MODEL, MLA_NOPE - SSD_HEADS), w_in.dtype),
        _pad_cols(w_in[:, o_kr:o_gate], 0, HEAD_PAD - MLA_QK)], axis=1)
    z, xbc, gate, qc, kvc, dtkr = _inproj(x, _row(p["mix_pre_g"][l]), w_in, w_in[:, o_gate:], w_small)

    d_skip_exp = jnp.repeat(p["d_skip"][l].astype(F32), SSD_HEAD_DIM).reshape(1, SSD_INNER)
    per_head_rows = lambda v: jnp.broadcast_to(v.astype(F32)[:, None], (SSD_HEADS, SSD_CHUNK))
    yn = _ssd(xbc, z, dtkr, p["conv_w"][l].astype(F32), _row(p["conv_b"][l]), per_head_rows(p["dt_bias"][l]),
              per_head_rows(p["a_log"][l]), d_skip_exp, _row(p["ssd_norm_g"][l]), batch, seq)

    inv = ROPE_THETA ** (-jnp.arange(0, MLA_ROPE, 2, dtype=F32) / MLA_ROPE)
    inv_col = inv.reshape(ROPE_HALF, 1)
    wq = p["w_uq"][l].reshape(MLA_Q_RANK, MLA_HEADS, MLA_QK)
    q_x1 = wq[..., MLA_NOPE:MLA_NOPE + ROPE_HALF]
    q_x2 = wq[..., MLA_NOPE + ROPE_HALF:]
    wuq_pad = jnp.concatenate([wq, -q_x2, q_x1], axis=-1).reshape(MLA_Q_RANK, MLA_HEADS * HEAD_PAD)
    wukv = jnp.concatenate([p["w_uk"][l].reshape(MLA_KV_RANK, MLA_HEADS, MLA_NOPE),
                            p["w_uv"][l].reshape(MLA_KV_RANK, MLA_HEADS, MLA_V)],
                           axis=-1).reshape(MLA_KV_RANK, MLA_HEADS * HEAD_PAD)
    qf, kf, vf = _mla_prep(qc, kvc, dtkr, pos, inv_col, _row(p["q_norm_g"][l]), _row(p["kv_norm_g"][l]),
                           bf(wuq_pad), bf(wukv))
    o = _attention(qf, kf, vf, batch, seq)

    w_mla = p["w_mla_proj"][l].reshape(MLA_HEADS // 2, 2, MLA_V, D_MODEL)[:, ::-1].reshape(MLA_HEADS * MLA_V, D_MODEL)
    x = _merge(x, yn, o, gate, _row(p["gate_bias"][l]), bf(p["w_ssd_proj"][l]), bf(w_mla),
               bf(p["w_out"][l]), _row(p["mix_post_g"][l]))

    mk, mv = _memkv(mem, _row(p["mem_norm_g"][l]), bf(p["w_xk"][l]), bf(p["w_xv"][l]))
    x = _xattn(x, _row(p["xa_pre_g"][l]), bf(p["w_xq"][l]), mk, mv, bf(p["w_xo"][l]), _row(p["xa_post_g"][l]),
               batch, seq, mem_len)

    return _ffn(x, _row(p["ffn2_pre_g"][l]), bf(p["ffn2_w_gate"][l]), bf(p["ffn2_w_up"][l]),
                bf(p["ffn2_w_down"][l]), _row(p["ffn2_post_g"][l]))


def kernel(x, mem, positions, ffn1_pre_g, ffn1_w_gate, ffn1_w_up, ffn1_w_down, ffn1_post_g, mix_pre_g, w_in, conv_w, conv_b, dt_bias, a_log, d_skip, ssd_norm_g, w_ssd_proj, q_norm_g, w_uq, kv_norm_g, w_uk, w_uv, w_mla_proj, gate_bias, w_out, mix_post_g, xa_pre_g, mem_norm_g, w_xq, w_xk, w_xv, w_xo, xa_post_g, ffn2_pre_g, ffn2_w_gate, ffn2_w_up, ffn2_w_down, ffn2_post_g):
    p = dict(ffn1_pre_g=ffn1_pre_g, ffn1_w_gate=ffn1_w_gate, ffn1_w_up=ffn1_w_up, ffn1_w_down=ffn1_w_down,
             ffn1_post_g=ffn1_post_g, mix_pre_g=mix_pre_g, w_in=w_in, conv_w=conv_w, conv_b=conv_b,
             dt_bias=dt_bias, a_log=a_log, d_skip=d_skip, ssd_norm_g=ssd_norm_g, w_ssd_proj=w_ssd_proj,
             q_norm_g=q_norm_g, w_uq=w_uq, kv_norm_g=kv_norm_g, w_uk=w_uk, w_uv=w_uv, w_mla_proj=w_mla_proj,
             gate_bias=gate_bias, w_out=w_out, mix_post_g=mix_post_g, xa_pre_g=xa_pre_g, mem_norm_g=mem_norm_g,
             w_xq=w_xq, w_xk=w_xk, w_xv=w_xv, w_xo=w_xo, xa_post_g=xa_post_g, ffn2_pre_g=ffn2_pre_g,
             ffn2_w_gate=ffn2_w_gate, ffn2_w_up=ffn2_w_up, ffn2_w_down=ffn2_w_down, ffn2_post_g=ffn2_post_g)
    batch, seq, _ = x.shape
    mem_len = mem.shape[1]
    assert seq % TS == 0 and seq % TM == 0 and seq % TQ == 0 and TS % SSD_CHUNK == 0 and batch % SSD_SEQS == 0
    xf = x.reshape(batch * seq, D_MODEL)
    memf = mem.reshape(batch * mem_len, D_MODEL)
    pos = positions.reshape(batch * seq // TM, 1, TM)
    for l in range(w_in.shape[0]):
        xf = _layer(xf, memf, pos, l, p, batch, seq, mem_len)
    return xf.reshape(batch, seq, D_MODEL)
```

```python
import math

import numpy as np
import jax
import jax.numpy as jnp
from jax import lax
from jax.experimental import pallas as pl
from jax.experimental.pallas import tpu as pltpu

F32 = jnp.float32
BF16 = jnp.bfloat16

D_MODEL = 1024
SSD_HEADS = 16
SSD_HEAD_DIM = 64
SSD_INNER = SSD_HEADS * SSD_HEAD_DIM
SSD_GROUPS = 2
SSD_STATE = 128
SSD_CONV = 4
SSD_CHUNK = 128
SSD_CONV_CH = SSD_INNER + 2 * SSD_GROUPS * SSD_STATE
MLA_HEADS = 16
MLA_Q_RANK = 384
MLA_KV_RANK = 256
MLA_NOPE = 64
MLA_ROPE = 32
MLA_V = 64
MLA_QK = MLA_NOPE + MLA_ROPE
ROPE_THETA = 10000.0
XA_HEADS = 4
XA_HEAD_DIM = D_MODEL // XA_HEADS
D_FF = 2816
FFN_RES_WEIGHT = 0.5
EPS = 1e-6

LANES = 128
BF16_ROWS = 16
HEAD_PAD = LANES
ROPE_HALF = MLA_ROPE // 2
GROUP_W = SSD_INNER // SSD_GROUPS
PAIRS_PER_GROUP = GROUP_W // LANES
LOG2E = math.log2(math.e)

TM = 1024
FFN_CHUNK = 256
TS = 512
SSD_SEQS = 4
TQ = 256
ATTN_LOOKAHEAD = 3
HIST = BF16_ROWS
VMEM_LIMIT = 56 * 1024 * 1024

_C_Z = (0, SSD_INNER)
_C_XBC = (_C_Z[1], _C_Z[1] + SSD_CONV_CH)
W_HEAD = _C_XBC[1]
_C_QC = (0, MLA_Q_RANK)
_C_KVC = (_C_QC[1], _C_QC[1] + MLA_KV_RANK)
_C_DTKR = (_C_KVC[1], _C_KVC[1] + LANES)
W_SMALL = _C_DTKR[1]

_ROWS_DT, _ROWS_ECS, _ROWS_DSF = (0, 32), (32, 64), (64, 96)
_ROWS_CS = (0, 48)

_HALVES = (slice(0, TM // 2), slice(TM // 2, TM))


def _rms(x, g):
    return x * lax.rsqrt(jnp.mean(x * x, axis=-1, keepdims=True) + EPS) * g


def _softplus(x):
    return jnp.maximum(x, 0.0) + jnp.log1p(jnp.exp(-jnp.abs(x)))


def _bf16_pieces(x, n):
    pieces, rest = [], x
    for _ in range(n):
        piece = rest.astype(BF16).astype(F32)
        pieces.append(piece)
        rest = rest - piece
    return pieces


def _resident(shape):
    zeros = (0,) * len(shape)
    return pl.BlockSpec(shape, lambda *_: zeros, pipeline_mode=pl.Buffered(1))


def _params(*sem):
    return pltpu.CompilerParams(dimension_semantics=sem, vmem_limit_bytes=VMEM_LIMIT)


def _ffn_kernel(x_ref, pre_g_ref, wg_ref, wu_ref, wd_ref, post_g_ref, o_ref, h_ref):
    xs = [x_ref[r, :] for r in _HALVES]
    xns = [_rms(x, pre_g_ref[...]).astype(BF16) for x in xs]
    for c in range(D_FF // FFN_CHUNK):
        sl = slice(c * FFN_CHUNK, (c + 1) * FFN_CHUNK)
        for r, xn in zip(_HALVES, xns):
            g = jnp.dot(xn, wg_ref[:, sl], preferred_element_type=F32)
            u = jnp.dot(xn, wu_ref[:, sl], preferred_element_type=F32)
            h_ref[r, sl] = (jax.nn.silu(g) * u).astype(BF16)
    ys = [jnp.dot(h_ref[r, :], wd_ref[...], preferred_element_type=F32) for r in _HALVES]
    for r, x, y in zip(_HALVES, xs, ys):
        o_ref[r, :] = x + FFN_RES_WEIGHT * _rms(y, post_g_ref[...])


def _ffn(x, pre_g, wg, wu, wd, post_g):
    t = x.shape[0]
    tok = pl.BlockSpec((TM, D_MODEL), lambda i: (i, 0))
    return pl.pallas_call(
        _ffn_kernel,
        grid=(t // TM,),
        in_specs=[tok, _resident((1, D_MODEL)), _resident((D_MODEL, D_FF)), _resident((D_MODEL, D_FF)),
                  _resident((D_FF, D_MODEL)), _resident((1, D_MODEL))],
        out_specs=tok,
        out_shape=jax.ShapeDtypeStruct((t, D_MODEL), F32),
        scratch_shapes=[pltpu.VMEM((TM, D_FF), BF16)],
        compiler_params=_params("parallel"),
        name="ffn",
    )(x, pre_g, wg, wu, wd, post_g)


def _inproj_kernel(x_ref, g_ref, whead_ref, wgate_ref, wsmall_ref, z_ref, xbc_ref, gate_ref, qc_ref, kvc_ref,
                   dtkr_ref):
    hns = [_rms(x_ref[r, :], g_ref[...]).astype(BF16) for r in _HALVES]
    outs = ((z_ref, whead_ref, _C_Z), (xbc_ref, whead_ref, _C_XBC), (gate_ref, wgate_ref, (0, 2 * D_MODEL)),
            (qc_ref, wsmall_ref, _C_QC), (kvc_ref, wsmall_ref, _C_KVC), (dtkr_ref, wsmall_ref, _C_DTKR))
    for out_ref, w_ref, cols in outs:
        for r, hn in zip(_HALVES, hns):
            out_ref[r, :] = jnp.dot(hn, w_ref[:, cols[0]:cols[1]],
                                    preferred_element_type=F32).astype(out_ref.dtype)


def _inproj(x, g, w_in_bf, w_gate, w_small):
    t = x.shape[0]

    def tok(width):
        return pl.BlockSpec((TM, width), lambda i: (i, 0))

    widths = (SSD_INNER, SSD_CONV_CH, 2 * D_MODEL, MLA_Q_RANK, MLA_KV_RANK, LANES)
    dtypes = (BF16, BF16, BF16, BF16, BF16, F32)
    return pl.pallas_call(
        _inproj_kernel,
        grid=(t // TM,),
        in_specs=[tok(D_MODEL), _resident((1, D_MODEL)), _resident((D_MODEL, W_HEAD)),
                  _resident((D_MODEL, 2 * D_MODEL)), _resident((D_MODEL, W_SMALL))],
        out_specs=[tok(w) for w in widths],
        out_shape=[jax.ShapeDtypeStruct((t, w), d) for w, d in zip(widths, dtypes)],
        compiler_params=_params("parallel"),
        name="inproj",
    )(x, g, w_in_bf, w_gate, w_small)


def _ssd_constants():
    L = SSD_CHUNK
    shift = np.zeros(((SSD_CONV - 1) * L, L + HIST), np.float32)
    for k in range(SSD_CONV - 1):
        shift[k * L + np.arange(L), np.arange(L) + HIST - (SSD_CONV - 1) + k] = 1.0
    head_of_lane = np.arange(SSD_INNER) // SSD_HEAD_DIM
    exp_ch = np.zeros((LANES, 3 * SSD_INNER), np.float32)
    for q, (lo, hi) in enumerate((_ROWS_DT, _ROWS_ECS, _ROWS_DSF)):
        for r in range(lo, hi):
            exp_ch[r, q * SSD_INNER + np.nonzero(head_of_lane == r % SSD_HEADS)[0]] = 1.0
    exp_blk = np.zeros((LANES, SSD_HEADS * LANES), np.float32)
    for r in range(*_ROWS_CS):
        h = r % SSD_HEADS
        exp_blk[r, h * LANES:(h + 1) * LANES] = 1.0
    tri = np.tile((np.arange(L)[:, None] <= np.arange(L)[None, :]).astype(np.float32), (3, 1))
    return tuple(jnp.asarray(m, BF16) for m in (shift, exp_ch, exp_blk, tri))


def _ssd_kernel(xbc_ref, z_ref, dt_ref, shift_ref, expch_ref, expblk_ref, tri_ref, cw_ref, cb_ref, dtb_ref,
                alog_ref, dskip_ref, ng_ref, y_ref, xpad, state):
    L = SSD_CHUNK

    @pl.when(pl.program_id(1) == 0)
    def _start_of_sequence():
        xpad[:, 0:HIST, :] = jnp.zeros((SSD_SEQS, HIST, SSD_CONV_CH), BF16)
        state[...] = jnp.zeros_like(state)

    xpad[:, HIST:HIST + TS, :] = xbc_ref[...]

    low_half = lax.broadcasted_iota(jnp.int32, (L, LANES), 1) < SSD_HEAD_DIM
    tril = lax.broadcasted_iota(jnp.int32, (L, L), 1) <= lax.broadcasted_iota(jnp.int32, (L, L), 0)
    a_neg = -jnp.exp(alog_ref[...])

    def to_token_rows(stack):
        full = jnp.concatenate([stack, jnp.zeros((LANES - stack.shape[0], L), F32)], axis=0)
        return full.T.astype(BF16)

    def chunk_stages(s, c):
        r0 = pl.multiple_of(c * L, L)
        rows = pl.ds(r0, L)
        win = xpad[s, pl.ds(r0, L + HIST), :]
        taps = jnp.dot(shift_ref[...], win, preferred_element_type=F32)
        yield
        dt_t = _softplus(dt_ref[s, rows, :].T[:SSD_HEADS, :] + dtb_ref[...])
        adt_t = dt_t * a_neg
        cs_t = jnp.dot(jnp.concatenate(_bf16_pieces(adt_t, 3), axis=1).astype(BF16), tri_ref[...],
                       preferred_element_type=F32)
        yield
        ecs_t = jnp.exp(cs_t)
        dsf_t = jnp.exp(cs_t[:, L - 1:L] - cs_t)
        stack = jnp.concatenate(_bf16_pieces(dt_t, 2) + _bf16_pieces(ecs_t, 2) + _bf16_pieces(dsf_t, 2), axis=0)
        per_ch = jnp.dot(to_token_rows(stack), expch_ref[...], preferred_element_type=F32)
        dt_e = per_ch[:, :SSD_INNER]
        ecs_e = per_ch[:, SSD_INNER:2 * SSD_INNER]
        dsf_e = per_ch[:, 2 * SSD_INNER:]
        cs_col = jnp.dot(to_token_rows(jnp.concatenate(_bf16_pieces(cs_t, 3), axis=0)), expblk_ref[...],
                         preferred_element_type=F32)
        yield
        u = cb_ref[...] + cw_ref[SSD_CONV - 1:SSD_CONV, :] * win[HIST:, :].astype(F32)
        for k in range(SSD_CONV - 1):
            u = u + cw_ref[k:k + 1, :] * taps[k * L:(k + 1) * L, :]
        u = jax.nn.silu(u)
        xs = u[:, :SSD_INNER]
        bmat = u[:, SSD_INNER:SSD_INNER + SSD_GROUPS * SSD_STATE]
        cmat = u[:, SSD_INNER + SSD_GROUPS * SSD_STATE:]
        z = z_ref[s, rows, :].astype(F32)
        yield

        for g in range(SSD_GROUPS):
            gcols = slice(g * GROUP_W, (g + 1) * GROUP_W)
            bm_g = bmat[:, g * SSD_STATE:(g + 1) * SSD_STATE]
            cm_g = cmat[:, g * SSD_STATE:(g + 1) * SSD_STATE].astype(BF16)
            cbm = lax.dot_general(cm_g, bm_g.astype(BF16), (((1,), (1,)), ((), ())),
                                  preferred_element_type=F32)
            s_g = state[s, g]
            y_off = jnp.dot(cm_g, s_g.astype(BF16), preferred_element_type=F32)
            xs_g = xs[:, gcols]
            xdt_g = xs_g * dt_e[:, gcols]
            xw_g = (xdt_g * dsf_e[:, gcols]).astype(BF16)
            state[s, g] = (s_g * ecs_e[L - 1:L, gcols]
                           + jnp.dot(bm_g.T.astype(BF16), xw_g, preferred_element_type=F32))
            yield
            y_parts = []
            for pp in range(PAIRS_PER_GROUP):
                p = g * PAIRS_PER_GROUP + pp

                def masked(h):
                    dec = jnp.exp(jnp.where(tril, cs_col[:, h * LANES:(h + 1) * LANES] - cs_t[h:h + 1, :],
                                            -jnp.inf))
                    return (cbm * dec).astype(BF16)

                xdt_p = xdt_g[:, pp * LANES:(pp + 1) * LANES]
                m2 = jnp.concatenate([masked(2 * p), masked(2 * p + 1)], axis=1)
                x_blk = jnp.concatenate([jnp.where(low_half, xdt_p, 0.0), jnp.where(low_half, 0.0, xdt_p)],
                                        axis=0).astype(BF16)
                y_parts.append(jnp.dot(m2, x_blk, preferred_element_type=F32))
                yield
            y_g = (jnp.concatenate(y_parts, axis=1) + y_off * ecs_e[:, gcols] + dskip_ref[:, gcols] * xs_g)
            y_g = y_g * jax.nn.silu(z[:, gcols])
            y_ref[s, rows, gcols] = _rms(y_g, ng_ref[:, gcols]).astype(BF16)
            yield

    def chunk(c, carry):
        for _ in zip(*[chunk_stages(s, c) for s in range(SSD_SEQS)]):
            pass
        return carry

    lax.fori_loop(0, TS // L, chunk, 0)
    xpad[:, 0:HIST, :] = xpad[:, TS:TS + HIST, :]


def _ssd(xbc, z, dt, conv_w, conv_b, dt_bias_t, a_log_t, d_skip_exp, norm_g, batch, seq):
    t = xbc.shape[0]
    nblk = seq // TS
    consts = _ssd_constants()

    def tok(width):
        return pl.BlockSpec((SSD_SEQS, TS, width), lambda b, j: (b, j, 0))

    by_seq = lambda a: a.reshape(batch, seq, a.shape[-1])
    y = pl.pallas_call(
        _ssd_kernel,
        grid=(batch // SSD_SEQS, nblk),
        in_specs=[tok(SSD_CONV_CH), tok(SSD_INNER), tok(LANES)] + [_resident(m.shape) for m in consts] + [
            _resident((SSD_CONV, SSD_CONV_CH)), _resident((1, SSD_CONV_CH)),
            _resident((SSD_HEADS, SSD_CHUNK)), _resident((SSD_HEADS, SSD_CHUNK)),
            _resident((1, SSD_INNER)), _resident((1, SSD_INNER))],
        out_specs=tok(SSD_INNER),
        out_shape=jax.ShapeDtypeStruct((batch, seq, SSD_INNER), BF16),
        scratch_shapes=[pltpu.VMEM((SSD_SEQS, TS + HIST, SSD_CONV_CH), BF16),
                        pltpu.VMEM((SSD_SEQS, SSD_GROUPS, SSD_STATE, GROUP_W), F32)],
        compiler_params=_params("parallel", "arbitrary"),
        name="ssd",
    )(by_seq(xbc), by_seq(z), by_seq(dt), *consts, conv_w, conv_b, dt_bias_t, a_log_t, d_skip_exp, norm_g)
    return y.reshape(t, SSD_INNER)


def _rope_constants():
    e = np.zeros((LANES, 2 * HEAD_PAD), np.float32)
    for piece in range(3):
        for j in range(ROPE_HALF):
            c_row = piece * ROPE_HALF + j
            s_row = 3 * ROPE_HALF + piece * ROPE_HALF + j
            e[c_row, MLA_NOPE + j] = e[c_row, MLA_NOPE + ROPE_HALF + j] = 1.0
            e[s_row, MLA_QK + j] = e[s_row, MLA_QK + ROPE_HALF + j] = 1.0
            e[s_row, HEAD_PAD + MLA_NOPE + j] = -1.0
            e[s_row, HEAD_PAD + MLA_NOPE + ROPE_HALF + j] = 1.0
    return jnp.asarray(e, BF16)


def _mla_prep_kernel(qc_ref, kvc_ref, kr_ref, pos_ref, inv_ref, erope_ref, qg_ref, kvg_ref, wuq_ref, wukv_ref,
                     q_out, k_out, v_out):
    ang_t = inv_ref[...] * pos_ref[0].astype(F32)
    stack = jnp.concatenate(_bf16_pieces(jnp.cos(ang_t), 3) + _bf16_pieces(jnp.sin(ang_t), 3)
                            + [jnp.zeros((LANES - 6 * ROPE_HALF, TM), F32)], axis=0)
    tabs = jnp.dot(stack.T.astype(BF16), erope_ref[...], preferred_element_type=F32)
    q_raw = tabs[:, :HEAD_PAD]
    k_sin = tabs[:, HEAD_PAD:]
    lane = lax.broadcasted_iota(jnp.int32, q_raw.shape, 1)
    first = (lane >= MLA_NOPE) & (lane < MLA_NOPE + ROPE_HALF)
    q_tab = jnp.where(lane < MLA_NOPE, 1.0, q_raw) * (MLA_QK ** -0.5 * LOG2E)

    low_lanes = lane < MLA_NOPE
    kr = kr_ref[...]
    swapped = jnp.where(first, pltpu.roll(kr, HEAD_PAD - ROPE_HALF, 1), pltpu.roll(kr, ROPE_HALF, 1))
    roped = kr * q_raw + swapped * k_sin
    k_rope = roped + pltpu.roll(roped, MLA_ROPE, 1)

    qn = _rms(qc_ref[...].astype(F32), qg_ref[...]).astype(BF16)
    kvn = _rms(kvc_ref[...].astype(F32), kvg_ref[...]).astype(BF16)
    for hp in range(MLA_HEADS // 2):
        cols2 = slice(2 * hp * HEAD_PAD, (2 * hp + 2) * HEAD_PAD)
        q2 = jnp.dot(qn, wuq_ref[:, cols2], preferred_element_type=F32)
        kv2 = jnp.dot(kvn, wukv_ref[:, cols2], preferred_element_type=F32)
        for hh in range(2):
            cols = slice((2 * hp + hh) * HEAD_PAD, (2 * hp + hh + 1) * HEAD_PAD)
            half = slice(hh * HEAD_PAD, (hh + 1) * HEAD_PAD)
            q_out[:, cols] = (q2[:, half] * q_tab).astype(BF16)
            k_out[:, cols] = jnp.where(low_lanes, kv2[:, half], k_rope).astype(BF16)
            v_out[:, cols] = jnp.where(low_lanes, 1.0, kv2[:, half]).astype(BF16)


def _mla_prep(qc, kvc, kr, pos_rows, inv_col, q_g, kv_g, wuq_pad, wukv):
    t = qc.shape[0]
    e_rope = _rope_constants()

    def tok(width):
        return pl.BlockSpec((TM, width), lambda i: (i, 0))

    wide = MLA_HEADS * HEAD_PAD
    return pl.pallas_call(
        _mla_prep_kernel,
        grid=(t // TM,),
        in_specs=[tok(MLA_Q_RANK), tok(MLA_KV_RANK), tok(HEAD_PAD), pl.BlockSpec((1, 1, TM), lambda i: (i, 0, 0)),
                  _resident((ROPE_HALF, 1)), _resident(e_rope.shape),
                  _resident((1, MLA_Q_RANK)), _resident((1, MLA_KV_RANK)),
                  _resident((MLA_Q_RANK, wide)), _resident((MLA_KV_RANK, wide))],
        out_specs=[tok(wide)] * 3,
        out_shape=[jax.ShapeDtypeStruct((t, wide), BF16)] * 3,
        compiler_params=_params("parallel"),
        name="mla_prep",
    )(qc, kvc, kr, pos_rows, inv_col, e_rope, q_g, kv_g, wuq_pad, wukv)


def _attn_kernel(q_ref, k_ref, v_ref, o_ref):
    seq = q_ref.shape[0]
    causal = (lax.broadcasted_iota(jnp.int32, (TQ, TQ), 1) <= lax.broadcasted_iota(jnp.int32, (TQ, TQ), 0))
    low_half = lax.broadcasted_iota(jnp.int32, (TQ, HEAD_PAD), 1) < MLA_V
    nt = (((1,), (1,)), ((), ()))
    units = [(i, hh) for i in reversed(range(seq // TQ)) for hh in range(2)]

    def scores(i, hh):
        rows = slice(i * TQ, (i + 1) * TQ)
        hcols = slice(hh * HEAD_PAD, (hh + 1) * HEAD_PAD)
        q = q_ref[rows, hcols]
        s_diag = jnp.where(causal, lax.dot_general(q, k_ref[rows, hcols], nt, preferred_element_type=F32),
                           -jnp.inf)
        s_past = lax.dot_general(q, k_ref[0:i * TQ, hcols], nt, preferred_element_type=F32) if i else None
        return s_diag, s_past

    def weighted_values(i, hh, s_diag, s_past):
        rows = slice(i * TQ, (i + 1) * TQ)
        hcols = slice(hh * HEAD_PAD, (hh + 1) * HEAD_PAD)
        m = jnp.max(s_diag, axis=1, keepdims=True)
        if i:
            m = jnp.maximum(m, jnp.max(s_past, axis=1, keepdims=True))
        acc = jnp.dot(jnp.exp2(s_diag - m).astype(BF16), v_ref[rows, hcols], preferred_element_type=F32)
        if i:
            acc = acc + jnp.dot(jnp.exp2(s_past - m).astype(BF16), v_ref[0:i * TQ, hcols],
                                preferred_element_type=F32)
        return acc

    pending = [scores(*unit) for unit in units[:ATTN_LOOKAHEAD]]
    accs = []
    for n, (i, hh) in enumerate(units):
        current = pending.pop(0)
        if n + ATTN_LOOKAHEAD < len(units):
            pending.append(scores(*units[n + ATTN_LOOKAHEAD]))
        accs.append(weighted_values(i, hh, *current))
        if hh == 1:
            mixed = jnp.where(low_half, accs[1], accs[0])
            rolled = pltpu.roll(jnp.where(low_half, accs[0], accs[1]), MLA_V, 1)
            num = jnp.where(low_half, rolled, mixed)
            den = jnp.where(low_half, mixed, rolled)
            o_ref[i * TQ:(i + 1) * TQ, :] = (num / den).astype(BF16)
            accs = []


def _attention(qf, kf, vf, batch, seq):
    t = qf.shape[0]
    pair = pl.BlockSpec((seq, 2 * HEAD_PAD), lambda b, hp: (b, hp))
    return pl.pallas_call(
        _attn_kernel,
        grid=(batch, MLA_HEADS // 2),
        in_specs=[pair, pair, pair],
        out_specs=pl.BlockSpec((seq, 2 * MLA_V), lambda b, hp: (b, hp)),
        out_shape=jax.ShapeDtypeStruct((t, MLA_HEADS * MLA_V), BF16),
        compiler_params=_params("parallel", "parallel"),
        name="mla_attention",
    )(qf, kf, vf)


def _memkv_kernel(mem_ref, g_ref, wk_ref, wv_ref, k_ref, v_ref):
    mn = _rms(mem_ref[...], g_ref[...]).astype(BF16)
    k_ref[...] = jnp.dot(mn, wk_ref[...], preferred_element_type=F32).astype(BF16)
    v_ref[...] = jnp.dot(mn, wv_ref[...], preferred_element_type=F32).astype(BF16)


def _memkv(mem, g, wk, wv):
    t = mem.shape[0]
    rows = math.gcd(TM, t)
    blk = pl.BlockSpec((rows, D_MODEL), lambda b: (b, 0))
    sq = _resident((D_MODEL, D_MODEL))
    return pl.pallas_call(
        _memkv_kernel,
        grid=(t // rows,),
        in_specs=[blk, _resident((1, D_MODEL)), sq, sq],
        out_specs=[blk, blk],
        out_shape=[jax.ShapeDtypeStruct((t, D_MODEL), BF16)] * 2,
        compiler_params=_params("parallel"),
        name="memkv",
    )(mem, g, wk, wv)


def _merge_xattn_kernel(x_ref, yn_ref, o_ref, gate_ref, gb_ref, wssd_ref, wmla_ref, wout_ref, mix_post_ref,
                        xa_pre_ref, wq_ref, k_ref, v_ref, wo_ref, xa_post_ref, out_ref):
    heads = [slice(h * XA_HEAD_DIM, (h + 1) * XA_HEAD_DIM) for h in range(XA_HEADS)]
    nt = (((1,), (1,)), ((), ()))
    mixed = []
    for r in _HALVES:
        y_ssd = jnp.dot(yn_ref[r, :], wssd_ref[...], preferred_element_type=F32)
        y_mla = jnp.dot(o_ref[r, :], wmla_ref[...], preferred_element_type=F32)
        gates = jax.nn.sigmoid(gate_ref[r, :].astype(F32) + gb_ref[...])
        mixed.append((gates[:, :D_MODEL] * y_ssd + gates[:, D_MODEL:] * y_mla).astype(BF16))
    hs = [jnp.dot(m, wout_ref[...], preferred_element_type=F32) for m in mixed]
    xs = [x_ref[r, :] + _rms(h, mix_post_ref[...]) for r, h in zip(_HALVES, hs)]
    hns = [_rms(x, xa_pre_ref[...]).astype(BF16) for x in xs]
    qs = [(jnp.dot(hn, wq_ref[...], preferred_element_type=F32) * (XA_HEAD_DIM ** -0.5 * LOG2E)).astype(BF16)
          for hn in hns]
    scores = [[lax.dot_general(q[:, cols], k_ref[:, cols], nt, preferred_element_type=F32) for cols in heads]
              for q in qs]
    attended = []
    for head_scores in scores:
        outs = []
        for cols, s in zip(heads, head_scores):
            p = jnp.exp2(s - jnp.max(s, axis=1, keepdims=True))
            l = jnp.sum(p, axis=1, keepdims=True)
            o = jnp.dot(p.astype(BF16), v_ref[:, cols], preferred_element_type=F32) / l
            outs.append(o.astype(BF16))
        attended.append(jnp.concatenate(outs, axis=1))
    ys = [jnp.dot(o, wo_ref[...], preferred_element_type=F32) for o in attended]
    for r, x, y in zip(_HALVES, xs, ys):
        out_ref[r, :] = x + _rms(y, xa_post_ref[...])


def _merge_xattn(x, yn, o, gate, gate_bias, wssd, wmla, wout, mix_post_g, xa_pre_g, wq, k, v, wo, xa_post_g,
                 batch, seq, mem_len):
    t = x.shape[0]
    nblk = seq // TM

    def tok(width):
        return pl.BlockSpec((TM, width), lambda b, j: (b * nblk + j, 0))

    kv = pl.BlockSpec((mem_len, D_MODEL), lambda b, j: (b, 0))
    sq = _resident((D_MODEL, D_MODEL))
    vec = _resident((1, D_MODEL))
    return pl.pallas_call(
        _merge_xattn_kernel,
        grid=(batch, nblk),
        in_specs=[tok(D_MODEL), tok(SSD_INNER), tok(MLA_HEADS * MLA_V), tok(2 * D_MODEL),
                  _resident((1, 2 * D_MODEL)), sq, sq, sq, vec, vec, sq, kv, kv, sq, vec],
        out_specs=tok(D_MODEL),
        out_shape=jax.ShapeDtypeStruct((t, D_MODEL), F32),
        compiler_params=_params("parallel", "arbitrary"),
        name="merge_xattn",
    )(x, yn, o, gate, gate_bias, wssd, wmla, wout, mix_post_g, xa_pre_g, wq, k, v, wo, xa_post_g)


def _row(v):
    return v.reshape(1, -1).astype(F32)


def _pad_cols(w, left, right):
    return jnp.pad(w, ((0, 0), (left, right)))


def _layer(x, mem, pos, l, p, batch, seq, mem_len):
    bf = lambda w: w.astype(BF16)

    x = _ffn(x, _row(p["ffn1_pre_g"][l]), bf(p["ffn1_w_gate"][l]), bf(p["ffn1_w_up"][l]),
             bf(p["ffn1_w_down"][l]), _row(p["ffn1_post_g"][l]))

    w_in = bf(p["w_in"][l])
    o_dt = SSD_INNER + SSD_CONV_CH
    o_qc = o_dt + SSD_HEADS
    o_kvc = o_qc + MLA_Q_RANK
    o_kr = o_kvc + MLA_KV_RANK
    o_gate = o_kr + MLA_ROPE
    w_small = jnp.concatenate([
        w_in[:, o_qc:o_kr], w_in[:, o_dt:o_qc], jnp.zeros((D_MODEL, MLA_NOPE - SSD_HEADS), w_in.dtype),
        _pad_cols(w_in[:, o_kr:o_gate], 0, HEAD_PAD - MLA_QK)], axis=1)
    z, xbc, gate, qc, kvc, dtkr = _inproj(x, _row(p["mix_pre_g"][l]), w_in, w_in[:, o_gate:], w_small)

    d_skip_exp = jnp.repeat(p["d_skip"][l].astype(F32), SSD_HEAD_DIM).reshape(1, SSD_INNER)
    per_head_rows = lambda v: jnp.broadcast_to(v.astype(F32)[:, None], (SSD_HEADS, SSD_CHUNK))
    yn = _ssd(xbc, z, dtkr, p["conv_w"][l].astype(F32), _row(p["conv_b"][l]), per_head_rows(p["dt_bias"][l]),
              per_head_rows(p["a_log"][l]), d_skip_exp, _row(p["ssd_norm_g"][l]), batch, seq)

    inv = ROPE_THETA ** (-jnp.arange(0, MLA_ROPE, 2, dtype=F32) / MLA_ROPE)
    inv_col = inv.reshape(ROPE_HALF, 1)
    wq = p["w_uq"][l].reshape(MLA_Q_RANK, MLA_HEADS, MLA_QK)
    q_x1 = wq[..., MLA_NOPE:MLA_NOPE + ROPE_HALF]
    q_x2 = wq[..., MLA_NOPE + ROPE_HALF:]
    wuq_pad = jnp.concatenate([wq, -q_x2, q_x1], axis=-1).reshape(MLA_Q_RANK, MLA_HEADS * HEAD_PAD)
    wukv = jnp.concatenate([p["w_uk"][l].reshape(MLA_KV_RANK, MLA_HEADS, MLA_NOPE),
                            p["w_uv"][l].reshape(MLA_KV_RANK, MLA_HEADS, MLA_V)],
                           axis=-1).reshape(MLA_KV_RANK, MLA_HEADS * HEAD_PAD)
    qf, kf, vf = _mla_prep(qc, kvc, dtkr, pos, inv_col, _row(p["q_norm_g"][l]), _row(p["kv_norm_g"][l]),
                           bf(wuq_pad), bf(wukv))
    o = _attention(qf, kf, vf, batch, seq)

    w_mla = p["w_mla_proj"][l].reshape(MLA_HEADS // 2, 2, MLA_V, D_MODEL)[:, ::-1].reshape(MLA_HEADS * MLA_V, D_MODEL)
    mk, mv = _memkv(mem, _row(p["mem_norm_g"][l]), bf(p["w_xk"][l]), bf(p["w_xv"][l]))
    x = _merge_xattn(x, yn, o, gate, _row(p["gate_bias"][l]), bf(p["w_ssd_proj"][l]), bf(w_mla),
                     bf(p["w_out"][l]), _row(p["mix_post_g"][l]), _row(p["xa_pre_g"][l]), bf(p["w_xq"][l]),
                     mk, mv, bf(p["w_xo"][l]), _row(p["xa_post_g"][l]), batch, seq, mem_len)

    return _ffn(x, _row(p["ffn2_pre_g"][l]), bf(p["ffn2_w_gate"][l]), bf(p["ffn2_w_up"][l]),
                bf(p["ffn2_w_down"][l]), _row(p["ffn2_post_g"][l]))


def kernel(x, mem, positions, ffn1_pre_g, ffn1_w_gate, ffn1_w_up, ffn1_w_down, ffn1_post_g, mix_pre_g, w_in, conv_w, conv_b, dt_bias, a_log, d_skip, ssd_norm_g, w_ssd_proj, q_norm_g, w_uq, kv_norm_g, w_uk, w_uv, w_mla_proj, gate_bias, w_out, mix_post_g, xa_pre_g, mem_norm_g, w_xq, w_xk, w_xv, w_xo, xa_post_g, ffn2_pre_g, ffn2_w_gate, ffn2_w_up, ffn2_w_down, ffn2_post_g):
    p = dict(ffn1_pre_g=ffn1_pre_g, ffn1_w_gate=ffn1_w_gate, ffn1_w_up=ffn1_w_up, ffn1_w_down=ffn1_w_down,
             ffn1_post_g=ffn1_post_g, mix_pre_g=mix_pre_g, w_in=w_in, conv_w=conv_w, conv_b=conv_b,
             dt_bias=dt_bias, a_log=a_log, d_skip=d_skip, ssd_norm_g=ssd_norm_g, w_ssd_proj=w_ssd_proj,
             q_norm_g=q_norm_g, w_uq=w_uq, kv_norm_g=kv_norm_g, w_uk=w_uk, w_uv=w_uv, w_mla_proj=w_mla_proj,
             gate_bias=gate_bias, w_out=w_out, mix_post_g=mix_post_g, xa_pre_g=xa_pre_g, mem_norm_g=mem_norm_g,
             w_xq=w_xq, w_xk=w_xk, w_xv=w_xv, w_xo=w_xo, xa_post_g=xa_post_g, ffn2_pre_g=ffn2_pre_g,
             ffn2_w_gate=ffn2_w_gate, ffn2_w_up=ffn2_w_up, ffn2_w_down=ffn2_w_down, ffn2_post_g=ffn2_post_g)
    batch, seq, _ = x.shape
    mem_len = mem.shape[1]
    assert seq % TS == 0 and seq % TM == 0 and seq % TQ == 0 and TS % SSD_CHUNK == 0 and batch % SSD_SEQS == 0
    xf = x.reshape(batch * seq, D_MODEL)
    memf = mem.reshape(batch * mem_len, D_MODEL)
    pos = positions.reshape(batch * seq // TM, 1, TM)
    for l in range(w_in.shape[0]):
        xf = _layer(xf, memf, pos, l, p, batch, seq, mem_len)
    return xf.reshape(batch, seq, D_MODEL)
```

```python
import math

import numpy as np
import jax
import jax.numpy as jnp
from jax import lax
from jax.experimental import pallas as pl
from jax.experimental.pallas import tpu as pltpu

F32 = jnp.float32
BF16 = jnp.bfloat16

D_MODEL = 1024
SSD_HEADS = 16
SSD_HEAD_DIM = 64
SSD_INNER = SSD_HEADS * SSD_HEAD_DIM
SSD_GROUPS = 2
SSD_STATE = 128
SSD_CONV = 4
SSD_CHUNK = 128
SSD_CONV_CH = SSD_INNER + 2 * SSD_GROUPS * SSD_STATE
MLA_HEADS = 16
MLA_Q_RANK = 384
MLA_KV_RANK = 256
MLA_NOPE = 64
MLA_ROPE = 32
MLA_V = 64
MLA_QK = MLA_NOPE + MLA_ROPE
ROPE_THETA = 10000.0
XA_HEADS = 4
XA_HEAD_DIM = D_MODEL // XA_HEADS
D_FF = 2816
FFN_RES_WEIGHT = 0.5
EPS = 1e-6

LANES = 128
BF16_ROWS = 16
HEAD_PAD = LANES
ROPE_HALF = MLA_ROPE // 2
GROUP_W = SSD_INNER // SSD_GROUPS
PAIRS_PER_GROUP = GROUP_W // LANES
LOG2E = math.log2(math.e)

TM = 1024
FFN_CHUNK = 256
TS = 512
SSD_SEQS = 4
TQ = 256
ATTN_LOOKAHEAD = 3
HIST = BF16_ROWS
VMEM_LIMIT = 56 * 1024 * 1024

_C_Z = (0, SSD_INNER)
_C_XBC = (_C_Z[1], _C_Z[1] + SSD_CONV_CH)
W_HEAD = _C_XBC[1]
_C_QC = (0, MLA_Q_RANK)
_C_KVC = (_C_QC[1], _C_QC[1] + MLA_KV_RANK)
_C_DTKR = (_C_KVC[1], _C_KVC[1] + LANES)
W_SMALL = _C_DTKR[1]

_ROWS_DT, _ROWS_ECS, _ROWS_DSF = (0, 32), (32, 64), (64, 96)
_ROWS_CS = (0, 48)

_HALVES = (slice(0, TM // 2), slice(TM // 2, TM))


def _rms(x, g):
    return x * lax.rsqrt(jnp.mean(x * x, axis=-1, keepdims=True) + EPS) * g


def _softplus(x):
    return jnp.maximum(x, 0.0) + jnp.log1p(jnp.exp(-jnp.abs(x)))


def _bf16_pieces(x, n):
    pieces, rest = [], x
    for _ in range(n):
        piece = rest.astype(BF16).astype(F32)
        pieces.append(piece)
        rest = rest - piece
    return pieces


def _resident(shape):
    zeros = (0,) * len(shape)
    return pl.BlockSpec(shape, lambda *_: zeros, pipeline_mode=pl.Buffered(1))


def _params(*sem):
    return pltpu.CompilerParams(dimension_semantics=sem, vmem_limit_bytes=VMEM_LIMIT)


def _ffn_kernel(x_ref, pre_g_ref, wg_ref, wu_ref, wd_ref, post_g_ref, o_ref, h_ref):
    xs = [x_ref[r, :] for r in _HALVES]
    xns = [_rms(x, pre_g_ref[...]).astype(BF16) for x in xs]
    for c in range(D_FF // FFN_CHUNK):
        sl = slice(c * FFN_CHUNK, (c + 1) * FFN_CHUNK)
        for r, xn in zip(_HALVES, xns):
            g = jnp.dot(xn, wg_ref[:, sl], preferred_element_type=F32)
            u = jnp.dot(xn, wu_ref[:, sl], preferred_element_type=F32)
            h_ref[r, sl] = (jax.nn.silu(g) * u).astype(BF16)
    ys = [jnp.dot(h_ref[r, :], wd_ref[...], preferred_element_type=F32) for r in _HALVES]
    for r, x, y in zip(_HALVES, xs, ys):
        o_ref[r, :] = x + FFN_RES_WEIGHT * _rms(y, post_g_ref[...])


def _ffn(x, pre_g, wg, wu, wd, post_g):
    t = x.shape[0]
    tok = pl.BlockSpec((TM, D_MODEL), lambda i: (i, 0))
    return pl.pallas_call(
        _ffn_kernel,
        grid=(t // TM,),
        in_specs=[tok, _resident((1, D_MODEL)), _resident((D_MODEL, D_FF)), _resident((D_MODEL, D_FF)),
                  _resident((D_FF, D_MODEL)), _resident((1, D_MODEL))],
        out_specs=tok,
        out_shape=jax.ShapeDtypeStruct((t, D_MODEL), F32),
        scratch_shapes=[pltpu.VMEM((TM, D_FF), BF16)],
        compiler_params=_params("parallel"),
        name="ffn",
    )(x, pre_g, wg, wu, wd, post_g)


def _inproj_kernel(x_ref, g_ref, whead_ref, wgate_ref, wsmall_ref, z_ref, xbc_ref, gate_ref, qc_ref, kvc_ref,
                   dtkr_ref):
    hns = [_rms(x_ref[r, :], g_ref[...]).astype(BF16) for r in _HALVES]
    outs = ((z_ref, whead_ref, _C_Z), (xbc_ref, whead_ref, _C_XBC), (gate_ref, wgate_ref, (0, 2 * D_MODEL)),
            (qc_ref, wsmall_ref, _C_QC), (kvc_ref, wsmall_ref, _C_KVC), (dtkr_ref, wsmall_ref, _C_DTKR))
    for out_ref, w_ref, cols in outs:
        for r, hn in zip(_HALVES, hns):
            out_ref[r, :] = jnp.dot(hn, w_ref[:, cols[0]:cols[1]],
                                    preferred_element_type=F32).astype(out_ref.dtype)


def _inproj(x, g, w_in_bf, w_gate, w_small):
    t = x.shape[0]

    def tok(width):
        return pl.BlockSpec((TM, width), lambda i: (i, 0))

    widths = (SSD_INNER, SSD_CONV_CH, 2 * D_MODEL, MLA_Q_RANK, MLA_KV_RANK, LANES)
    dtypes = (BF16, BF16, BF16, BF16, BF16, F32)
    return pl.pallas_call(
        _inproj_kernel,
        grid=(t // TM,),
        in_specs=[tok(D_MODEL), _resident((1, D_MODEL)), _resident((D_MODEL, W_HEAD)),
                  _resident((D_MODEL, 2 * D_MODEL)), _resident((D_MODEL, W_SMALL))],
        out_specs=[tok(w) for w in widths],
        out_shape=[jax.ShapeDtypeStruct((t, w), d) for w, d in zip(widths, dtypes)],
        compiler_params=_params("parallel"),
        name="inproj",
    )(x, g, w_in_bf, w_gate, w_small)


def _ssd_constants():
    L = SSD_CHUNK
    shift = np.zeros(((SSD_CONV - 1) * L, L + HIST), np.float32)
    for k in range(SSD_CONV - 1):
        shift[k * L + np.arange(L), np.arange(L) + HIST - (SSD_CONV - 1) + k] = 1.0
    head_of_lane = np.arange(SSD_INNER) // SSD_HEAD_DIM
    exp_ch = np.zeros((LANES, 3 * SSD_INNER), np.float32)
    for q, (lo, hi) in enumerate((_ROWS_DT, _ROWS_ECS, _ROWS_DSF)):
        for r in range(lo, hi):
            exp_ch[r, q * SSD_INNER + np.nonzero(head_of_lane == r % SSD_HEADS)[0]] = 1.0
    exp_blk = np.zeros((LANES, SSD_HEADS * LANES), np.float32)
    for r in range(*_ROWS_CS):
        h = r % SSD_HEADS
        exp_blk[r, h * LANES:(h + 1) * LANES] = 1.0
    tri = np.tile((np.arange(L)[:, None] <= np.arange(L)[None, :]).astype(np.float32), (3, 1))
    return tuple(jnp.asarray(m, BF16) for m in (shift, exp_ch, exp_blk, tri))


def _ssd_kernel(xbc_ref, z_ref, dt_ref, shift_ref, expch_ref, expblk_ref, tri_ref, cw_ref, cb_ref, dtb_ref,
                alog_ref, dskip_ref, ng_ref, y_ref, xpad, state):
    L = SSD_CHUNK

    @pl.when(pl.program_id(1) == 0)
    def _start_of_sequence():
        xpad[:, 0:HIST, :] = jnp.zeros((SSD_SEQS, HIST, SSD_CONV_CH), BF16)
        state[...] = jnp.zeros_like(state)

    xpad[:, HIST:HIST + TS, :] = xbc_ref[...]

    low_half = lax.broadcasted_iota(jnp.int32, (L, LANES), 1) < SSD_HEAD_DIM
    tril = lax.broadcasted_iota(jnp.int32, (L, L), 1) <= lax.broadcasted_iota(jnp.int32, (L, L), 0)
    a_neg = -jnp.exp(alog_ref[...])

    def to_token_rows(stack):
        full = jnp.concatenate([stack, jnp.zeros((LANES - stack.shape[0], L), F32)], axis=0)
        return full.T.astype(BF16)

    def chunk_stages(s, c):
        r0 = pl.multiple_of(c * L, L)
        rows = pl.ds(r0, L)
        win = xpad[s, pl.ds(r0, L + HIST), :]
        taps = jnp.dot(shift_ref[...], win, preferred_element_type=F32)
        yield
        dt_t = _softplus(dt_ref[s, rows, :].T[:SSD_HEADS, :] + dtb_ref[...])
        adt_t = dt_t * a_neg
        cs_t = jnp.dot(jnp.concatenate(_bf16_pieces(adt_t, 3), axis=1).astype(BF16), tri_ref[...],
                       preferred_element_type=F32)
        yield
        ecs_t = jnp.exp(cs_t)
        dsf_t = jnp.exp(cs_t[:, L - 1:L] - cs_t)
        stack = jnp.concatenate(_bf16_pieces(dt_t, 2) + _bf16_pieces(ecs_t, 2) + _bf16_pieces(dsf_t, 2), axis=0)
        stack_rows = to_token_rows(stack)
        cs_rows = to_token_rows(jnp.concatenate(_bf16_pieces(cs_t, 3), axis=0))
        yield
        u = cb_ref[...] + cw_ref[SSD_CONV - 1:SSD_CONV, :] * win[HIST:, :].astype(F32)
        for k in range(SSD_CONV - 1):
            u = u + cw_ref[k:k + 1, :] * taps[k * L:(k + 1) * L, :]
        u = jax.nn.silu(u)
        xs = u[:, :SSD_INNER]
        bmat = u[:, SSD_INNER:SSD_INNER + SSD_GROUPS * SSD_STATE]
        cmat = u[:, SSD_INNER + SSD_GROUPS * SSD_STATE:]
        z = z_ref[s, rows, :].astype(F32)
        yield

        for g in range(SSD_GROUPS):
            gcols = slice(g * GROUP_W, (g + 1) * GROUP_W)
            dt_e, ecs_e, dsf_e = [jnp.dot(stack_rows, expch_ref[:, q * SSD_INNER + g * GROUP_W:
                                                                  q * SSD_INNER + (g + 1) * GROUP_W],
                                          preferred_element_type=F32) for q in range(3)]
            heads_g = SSD_HEADS // SSD_GROUPS
            cs_col = jnp.dot(cs_rows, expblk_ref[:, g * heads_g * LANES:(g + 1) * heads_g * LANES],
                             preferred_element_type=F32)
            bm_g = bmat[:, g * SSD_STATE:(g + 1) * SSD_STATE]
            cm_g = cmat[:, g * SSD_STATE:(g + 1) * SSD_STATE].astype(BF16)
            cbm = lax.dot_general(cm_g, bm_g.astype(BF16), (((1,), (1,)), ((), ())),
                                  preferred_element_type=F32)
            s_g = state[s, g]
            y_off = jnp.dot(cm_g, s_g.astype(BF16), preferred_element_type=F32)
            xs_g = xs[:, gcols]
            xdt_g = xs_g * dt_e
            xw_g = (xdt_g * dsf_e).astype(BF16)
            state[s, g] = (s_g * ecs_e[L - 1:L, :]
                           + jnp.dot(bm_g.T.astype(BF16), xw_g, preferred_element_type=F32))
            yield
            y_parts = []
            for pp in range(PAIRS_PER_GROUP):
                p = g * PAIRS_PER_GROUP + pp

                def masked(h):
                    dec = jnp.exp(jnp.where(tril, cs_col[:, (h % heads_g) * LANES:(h % heads_g + 1) * LANES] - cs_t[h:h + 1, :],
                                            -jnp.inf))
                    return (cbm * dec).astype(BF16)

                xdt_p = xdt_g[:, pp * LANES:(pp + 1) * LANES]
                m2 = jnp.concatenate([masked(2 * p), masked(2 * p + 1)], axis=1)
                x_blk = jnp.concatenate([jnp.where(low_half, xdt_p, 0.0), jnp.where(low_half, 0.0, xdt_p)],
                                        axis=0).astype(BF16)
                y_parts.append(jnp.dot(m2, x_blk, preferred_element_type=F32))
                yield
            y_g = (jnp.concatenate(y_parts, axis=1) + y_off * ecs_e + dskip_ref[:, gcols] * xs_g)
            y_g = y_g * jax.nn.silu(z[:, gcols])
            y_ref[s, rows, gcols] = _rms(y_g, ng_ref[:, gcols]).astype(BF16)
            yield

    def chunk(c, carry):
        for _ in zip(*[chunk_stages(s, c) for s in range(SSD_SEQS)]):
            pass
        return carry

    lax.fori_loop(0, TS // L, chunk, 0)
    xpad[:, 0:HIST, :] = xpad[:, TS:TS + HIST, :]


def _ssd(xbc, z, dt, conv_w, conv_b, dt_bias_t, a_log_t, d_skip_exp, norm_g, batch, seq):
    t = xbc.shape[0]
    nblk = seq // TS
    consts = _ssd_constants()

    def tok(width):
        return pl.BlockSpec((SSD_SEQS, TS, width), lambda b, j: (b, j, 0))

    by_seq = lambda a: a.reshape(batch, seq, a.shape[-1])
    y = pl.pallas_call(
        _ssd_kernel,
        grid=(batch // SSD_SEQS, nblk),
        in_specs=[tok(SSD_CONV_CH), tok(SSD_INNER), tok(LANES)] + [_resident(m.shape) for m in consts] + [
            _resident((SSD_CONV, SSD_CONV_CH)), _resident((1, SSD_CONV_CH)),
            _resident((SSD_HEADS, SSD_CHUNK)), _resident((SSD_HEADS, SSD_CHUNK)),
            _resident((1, SSD_INNER)), _resident((1, SSD_INNER))],
        out_specs=tok(SSD_INNER),
        out_shape=jax.ShapeDtypeStruct((batch, seq, SSD_INNER), BF16),
        scratch_shapes=[pltpu.VMEM((SSD_SEQS, TS + HIST, SSD_CONV_CH), BF16),
                        pltpu.VMEM((SSD_SEQS, SSD_GROUPS, SSD_STATE, GROUP_W), F32)],
        compiler_params=_params("parallel", "arbitrary"),
        name="ssd",
    )(by_seq(xbc), by_seq(z), by_seq(dt), *consts, conv_w, conv_b, dt_bias_t, a_log_t, d_skip_exp, norm_g)
    return y.reshape(t, SSD_INNER)


def _rope_constants():
    e = np.zeros((LANES, 2 * HEAD_PAD), np.float32)
    for piece in range(3):
        for j in range(ROPE_HALF):
            c_row = piece * ROPE_HALF + j
            s_row = 3 * ROPE_HALF + piece * ROPE_HALF + j
            e[c_row, MLA_NOPE + j] = e[c_row, MLA_NOPE + ROPE_HALF + j] = 1.0
            e[s_row, MLA_QK + j] = e[s_row, MLA_QK + ROPE_HALF + j] = 1.0
            e[s_row, HEAD_PAD + MLA_NOPE + j] = -1.0
            e[s_row, HEAD_PAD + MLA_NOPE + ROPE_HALF + j] = 1.0
    return jnp.asarray(e, BF16)


def _mla_prep_kernel(qc_ref, kvc_ref, kr_ref, pos_ref, inv_ref, erope_ref, qg_ref, kvg_ref, wuq_ref, wukv_ref,
                     q_out, k_out, v_out):
    ang_t = inv_ref[...] * pos_ref[0].astype(F32)
    stack = jnp.concatenate(_bf16_pieces(jnp.cos(ang_t), 3) + _bf16_pieces(jnp.sin(ang_t), 3)
                            + [jnp.zeros((LANES - 6 * ROPE_HALF, TM), F32)], axis=0)
    tabs = jnp.dot(stack.T.astype(BF16), erope_ref[...], preferred_element_type=F32)
    q_raw = tabs[:, :HEAD_PAD]
    k_sin = tabs[:, HEAD_PAD:]
    lane = lax.broadcasted_iota(jnp.int32, q_raw.shape, 1)
    first = (lane >= MLA_NOPE) & (lane < MLA_NOPE + ROPE_HALF)
    q_tab = jnp.where(lane < MLA_NOPE, 1.0, q_raw) * (MLA_QK ** -0.5 * LOG2E)

    low_lanes = lane < MLA_NOPE
    kr = kr_ref[...]
    swapped = jnp.where(first, pltpu.roll(kr, HEAD_PAD - ROPE_HALF, 1), pltpu.roll(kr, ROPE_HALF, 1))
    roped = kr * q_raw + swapped * k_sin
    k_rope = roped + pltpu.roll(roped, MLA_ROPE, 1)

    qn = _rms(qc_ref[...].astype(F32), qg_ref[...]).astype(BF16)
    kvn = _rms(kvc_ref[...].astype(F32), kvg_ref[...]).astype(BF16)
    for hp in range(MLA_HEADS // 2):
        cols2 = slice(2 * hp * HEAD_PAD, (2 * hp + 2) * HEAD_PAD)
        q2 = jnp.dot(qn, wuq_ref[:, cols2], preferred_element_type=F32)
        kv2 = jnp.dot(kvn, wukv_ref[:, cols2], preferred_element_type=F32)
        for hh in range(2):
            cols = slice((2 * hp + hh) * HEAD_PAD, (2 * hp + hh + 1) * HEAD_PAD)
            half = slice(hh * HEAD_PAD, (hh + 1) * HEAD_PAD)
            q_out[:, cols] = (q2[:, half] * q_tab).astype(BF16)
            k_out[:, cols] = jnp.where(low_lanes, kv2[:, half], k_rope).astype(BF16)
            v_out[:, cols] = jnp.where(low_lanes, 1.0, kv2[:, half]).astype(BF16)


def _mla_prep(qc, kvc, kr, pos_rows, inv_col, q_g, kv_g, wuq_pad, wukv):
    t = qc.shape[0]
    e_rope = _rope_constants()

    def tok(width):
        return pl.BlockSpec((TM, width), lambda i: (i, 0))

    wide = MLA_HEADS * HEAD_PAD
    return pl.pallas_call(
        _mla_prep_kernel,
        grid=(t // TM,),
        in_specs=[tok(MLA_Q_RANK), tok(MLA_KV_RANK), tok(HEAD_PAD), pl.BlockSpec((1, 1, TM), lambda i: (i, 0, 0)),
                  _resident((ROPE_HALF, 1)), _resident(e_rope.shape),
                  _resident((1, MLA_Q_RANK)), _resident((1, MLA_KV_RANK)),
                  _resident((MLA_Q_RANK, wide)), _resident((MLA_KV_RANK, wide))],
        out_specs=[tok(wide)] * 3,
        out_shape=[jax.ShapeDtypeStruct((t, wide), BF16)] * 3,
        compiler_params=_params("parallel"),
        name="mla_prep",
    )(qc, kvc, kr, pos_rows, inv_col, e_rope, q_g, kv_g, wuq_pad, wukv)


def _attn_kernel(q_ref, k_ref, v_ref, o_ref):
    seq = q_ref.shape[0]
    causal = (lax.broadcasted_iota(jnp.int32, (TQ, TQ), 1) <= lax.broadcasted_iota(jnp.int32, (TQ, TQ), 0))
    low_half = lax.broadcasted_iota(jnp.int32, (TQ, HEAD_PAD), 1) < MLA_V
    nt = (((1,), (1,)), ((), ()))
    units = [(i, hh) for i in reversed(range(seq // TQ)) for hh in range(2)]

    def scores(i, hh):
        rows = slice(i * TQ, (i + 1) * TQ)
        hcols = slice(hh * HEAD_PAD, (hh + 1) * HEAD_PAD)
        q = q_ref[rows, hcols]
        s_diag = jnp.where(causal, lax.dot_general(q, k_ref[rows, hcols], nt, preferred_element_type=F32),
                           -jnp.inf)
        s_past = lax.dot_general(q, k_ref[0:i * TQ, hcols], nt, preferred_element_type=F32) if i else None
        return s_diag, s_past

    def weighted_values(i, hh, s_diag, s_past):
        rows = slice(i * TQ, (i + 1) * TQ)
        hcols = slice(hh * HEAD_PAD, (hh + 1) * HEAD_PAD)
        m = jnp.max(s_diag, axis=1, keepdims=True)
        if i:
            m = jnp.maximum(m, jnp.max(s_past, axis=1, keepdims=True))
        acc = jnp.dot(jnp.exp2(s_diag - m).astype(BF16), v_ref[rows, hcols], preferred_element_type=F32)
        if i:
            acc = acc + jnp.dot(jnp.exp2(s_past - m).astype(BF16), v_ref[0:i * TQ, hcols],
                                preferred_element_type=F32)
        return acc

    pending = [scores(*unit) for unit in units[:ATTN_LOOKAHEAD]]
    accs = []
    for n, (i, hh) in enumerate(units):
        current = pending.pop(0)
        if n + ATTN_LOOKAHEAD < len(units):
            pending.append(scores(*units[n + ATTN_LOOKAHEAD]))
        accs.append(weighted_values(i, hh, *current))
        if hh == 1:
            mixed = jnp.where(low_half, accs[1], accs[0])
            rolled = pltpu.roll(jnp.where(low_half, accs[0], accs[1]), MLA_V, 1)
            num = jnp.where(low_half, rolled, mixed)
            den = jnp.where(low_half, mixed, rolled)
            o_ref[i * TQ:(i + 1) * TQ, :] = (num / den).astype(BF16)
            accs = []


def _attention(qf, kf, vf, batch, seq):
    t = qf.shape[0]
    pair = pl.BlockSpec((seq, 2 * HEAD_PAD), lambda b, hp: (b, hp))
    return pl.pallas_call(
        _attn_kernel,
        grid=(batch, MLA_HEADS // 2),
        in_specs=[pair, pair, pair],
        out_specs=pl.BlockSpec((seq, 2 * MLA_V), lambda b, hp: (b, hp)),
        out_shape=jax.ShapeDtypeStruct((t, MLA_HEADS * MLA_V), BF16),
        compiler_params=_params("parallel", "parallel"),
        name="mla_attention",
    )(qf, kf, vf)


def _memkv_kernel(mem_ref, g_ref, wk_ref, wv_ref, k_ref, v_ref):
    mn = _rms(mem_ref[...], g_ref[...]).astype(BF16)
    k_ref[...] = jnp.dot(mn, wk_ref[...], preferred_element_type=F32).astype(BF16)
    v_ref[...] = jnp.dot(mn, wv_ref[...], preferred_element_type=F32).astype(BF16)


def _memkv(mem, g, wk, wv):
    t = mem.shape[0]
    rows = math.gcd(TM, t)
    blk = pl.BlockSpec((rows, D_MODEL), lambda b: (b, 0))
    sq = _resident((D_MODEL, D_MODEL))
    return pl.pallas_call(
        _memkv_kernel,
        grid=(t // rows,),
        in_specs=[blk, _resident((1, D_MODEL)), sq, sq],
        out_specs=[blk, blk],
        out_shape=[jax.ShapeDtypeStruct((t, D_MODEL), BF16)] * 2,
        compiler_params=_params("parallel"),
        name="memkv",
    )(mem, g, wk, wv)


def _merge_xattn_kernel(x_ref, yn_ref, o_ref, gate_ref, gb_ref, wssd_ref, wmla_ref, wout_ref, mix_post_ref,
                        xa_pre_ref, wq_ref, k_ref, v_ref, wo_ref, xa_post_ref, out_ref):
    heads = [slice(h * XA_HEAD_DIM, (h + 1) * XA_HEAD_DIM) for h in range(XA_HEADS)]
    nt = (((1,), (1,)), ((), ()))
    mixed = []
    for r in _HALVES:
        y_ssd = jnp.dot(yn_ref[r, :], wssd_ref[...], preferred_element_type=F32)
        y_mla = jnp.dot(o_ref[r, :], wmla_ref[...], preferred_element_type=F32)
        gates = jax.nn.sigmoid(gate_ref[r, :].astype(F32) + gb_ref[...])
        mixed.append((gates[:, :D_MODEL] * y_ssd + gates[:, D_MODEL:] * y_mla).astype(BF16))
    hs = [jnp.dot(m, wout_ref[...], preferred_element_type=F32) for m in mixed]
    xs = [x_ref[r, :] + _rms(h, mix_post_ref[...]) for r, h in zip(_HALVES, hs)]
    hns = [_rms(x, xa_pre_ref[...]).astype(BF16) for x in xs]
    qs = [(jnp.dot(hn, wq_ref[...], preferred_element_type=F32) * (XA_HEAD_DIM ** -0.5 * LOG2E)).astype(BF16)
          for hn in hns]
    scores = [[lax.dot_general(q[:, cols], k_ref[:, cols], nt, preferred_element_type=F32) for cols in heads]
              for q in qs]
    attended = []
    for head_scores in scores:
        outs = []
        for cols, s in zip(heads, head_scores):
            p = jnp.exp2(s - jnp.max(s, axis=1, keepdims=True))
            l = jnp.sum(p, axis=1, keepdims=True)
            o = jnp.dot(p.astype(BF16), v_ref[:, cols], preferred_element_type=F32) / l
            outs.append(o.astype(BF16))
        attended.append(jnp.concatenate(outs, axis=1))
    ys = [jnp.dot(o, wo_ref[...], preferred_element_type=F32) for o in attended]
    for r, x, y in zip(_HALVES, xs, ys):
        out_ref[r, :] = x + _rms(y, xa_post_ref[...])


def _merge_xattn(x, yn, o, gate, gate_bias, wssd, wmla, wout, mix_post_g, xa_pre_g, wq, k, v, wo, xa_post_g,
                 batch, seq, mem_len):
    t = x.shape[0]
    nblk = seq // TM

    def tok(width):
        return pl.BlockSpec((TM, width), lambda b, j: (b * nblk + j, 0))

    kv = pl.BlockSpec((mem_len, D_MODEL), lambda b, j: (b, 0))
    sq = _resident((D_MODEL, D_MODEL))
    vec = _resident((1, D_MODEL))
    return pl.pallas_call(
        _merge_xattn_kernel,
        grid=(batch, nblk),
        in_specs=[tok(D_MODEL), tok(SSD_INNER), tok(MLA_HEADS * MLA_V), tok(2 * D_MODEL),
                  _resident((1, 2 * D_MODEL)), sq, sq, sq, vec, vec, sq, kv, kv, sq, vec],
        out_specs=tok(D_MODEL),
        out_shape=jax.ShapeDtypeStruct((t, D_MODEL), F32),
        compiler_params=_params("parallel", "arbitrary"),
        name="merge_xattn",
    )(x, yn, o, gate, gate_bias, wssd, wmla, wout, mix_post_g, xa_pre_g, wq, k, v, wo, xa_post_g)


def _row(v):
    return v.reshape(1, -1).astype(F32)


def _pad_cols(w, left, right):
    return jnp.pad(w, ((0, 0), (left, right)))


def _layer(x, mem, pos, l, p, batch, seq, mem_len):
    bf = lambda w: w.astype(BF16)

    x = _ffn(x, _row(p["ffn1_pre_g"][l]), bf(p["ffn1_w_gate"][l]), bf(p["ffn1_w_up"][l]),
             bf(p["ffn1_w_down"][l]), _row(p["ffn1_post_g"][l]))

    w_in = bf(p["w_in"][l])
    o_dt = SSD_INNER + SSD_CONV_CH
    o_qc = o_dt + SSD_HEADS
    o_kvc = o_qc + MLA_Q_RANK
    o_kr = o_kvc + MLA_KV_RANK
    o_gate = o_kr + MLA_ROPE
    w_small = jnp.concatenate([
        w_in[:, o_qc:o_kr], w_in[:, o_dt:o_qc], jnp.zeros((D_MODEL, MLA_NOPE - SSD_HEADS), w_in.dtype),
        _pad_cols(w_in[:, o_kr:o_gate], 0, HEAD_PAD - MLA_QK)], axis=1)
    z, xbc, gate, qc, kvc, dtkr = _inproj(x, _row(p["mix_pre_g"][l]), w_in, w_in[:, o_gate:], w_small)

    d_skip_exp = jnp.repeat(p["d_skip"][l].astype(F32), SSD_HEAD_DIM).reshape(1, SSD_INNER)
    per_head_rows = lambda v: jnp.broadcast_to(v.astype(F32)[:, None], (SSD_HEADS, SSD_CHUNK))
    yn = _ssd(xbc, z, dtkr, p["conv_w"][l].astype(F32), _row(p["conv_b"][l]), per_head_rows(p["dt_bias"][l]),
              per_head_rows(p["a_log"][l]), d_skip_exp, _row(p["ssd_norm_g"][l]), batch, seq)

    inv = ROPE_THETA ** (-jnp.arange(0, MLA_ROPE, 2, dtype=F32) / MLA_ROPE)
    inv_col = inv.reshape(ROPE_HALF, 1)
    wq = p["w_uq"][l].reshape(MLA_Q_RANK, MLA_HEADS, MLA_QK)
    q_x1 = wq[..., MLA_NOPE:MLA_NOPE + ROPE_HALF]
    q_x2 = wq[..., MLA_NOPE + ROPE_HALF:]
    wuq_pad = jnp.concatenate([wq, -q_x2, q_x1], axis=-1).reshape(MLA_Q_RANK, MLA_HEADS * HEAD_PAD)
    wukv = jnp.concatenate([p["w_uk"][l].reshape(MLA_KV_RANK, MLA_HEADS, MLA_NOPE),
                            p["w_uv"][l].reshape(MLA_KV_RANK, MLA_HEADS, MLA_V)],
                           axis=-1).reshape(MLA_KV_RANK, MLA_HEADS * HEAD_PAD)
    qf, kf, vf = _mla_prep(qc, kvc, dtkr, pos, inv_col, _row(p["q_norm_g"][l]), _row(p["kv_norm_g"][l]),
                           bf(wuq_pad), bf(wukv))
    o = _attention(qf, kf, vf, batch, seq)

    w_mla = p["w_mla_proj"][l].reshape(MLA_HEADS // 2, 2, MLA_V, D_MODEL)[:, ::-1].reshape(MLA_HEADS * MLA_V, D_MODEL)
    mk, mv = _memkv(mem, _row(p["mem_norm_g"][l]), bf(p["w_xk"][l]), bf(p["w_xv"][l]))
    x = _merge_xattn(x, yn, o, gate, _row(p["gate_bias"][l]), bf(p["w_ssd_proj"][l]), bf(w_mla),
                     bf(p["w_out"][l]), _row(p["mix_post_g"][l]), _row(p["xa_pre_g"][l]), bf(p["w_xq"][l]),
                     mk, mv, bf(p["w_xo"][l]), _row(p["xa_post_g"][l]), batch, seq, mem_len)

    return _ffn(x, _row(p["ffn2_pre_g"][l]), bf(p["ffn2_w_gate"][l]), bf(p["ffn2_w_up"][l]),
                bf(p["ffn2_w_down"][l]), _row(p["ffn2_post_g"][l]))


def kernel(x, mem, positions, ffn1_pre_g, ffn1_w_gate, ffn1_w_up, ffn1_w_down, ffn1_post_g, mix_pre_g, w_in, conv_w, conv_b, dt_bias, a_log, d_skip, ssd_norm_g, w_ssd_proj, q_norm_g, w_uq, kv_norm_g, w_uk, w_uv, w_mla_proj, gate_bias, w_out, mix_post_g, xa_pre_g, mem_norm_g, w_xq, w_xk, w_xv, w_xo, xa_post_g, ffn2_pre_g, ffn2_w_gate, ffn2_w_up, ffn2_w_down, ffn2_post_g):
    p = dict(ffn1_pre_g=ffn1_pre_g, ffn1_w_gate=ffn1_w_gate, ffn1_w_up=ffn1_w_up, ffn1_w_down=ffn1_w_down,
             ffn1_post_g=ffn1_post_g, mix_pre_g=mix_pre_g, w_in=w_in, conv_w=conv_w, conv_b=conv_b,
             dt_bias=dt_bias, a_log=a_log, d_skip=d_skip, ssd_norm_g=ssd_norm_g, w_ssd_proj=w_ssd_proj,
             q_norm_g=q_norm_g, w_uq=w_uq, kv_norm_g=kv_norm_g, w_uk=w_uk, w_uv=w_uv, w_mla_proj=w_mla_proj,
             gate_bias=gate_bias, w_out=w_out, mix_post_g=mix_post_g, xa_pre_g=xa_pre_g, mem_norm_g=mem_norm_g,
             w_xq=w_xq, w_xk=w_xk, w_xv=w_xv, w_xo=w_xo, xa_post_g=xa_post_g, ffn2_pre_g=ffn2_pre_g,
             ffn2_w_gate=ffn2_w_gate, ffn2_w_up=ffn2_w_up, ffn2_w_down=ffn2_w_down, ffn2_post_g=ffn2_post_g)
    batch, seq, _ = x.shape
    mem_len = mem.shape[1]
    assert seq % TS == 0 and seq % TM == 0 and seq % TQ == 0 and TS % SSD_CHUNK == 0 and batch % SSD_SEQS == 0
    xf = x.reshape(batch * seq, D_MODEL)
    memf = mem.reshape(batch * mem_len, D_MODEL)
    pos = positions.reshape(batch * seq // TM, 1, TM)
    for l in range(w_in.shape[0]):
        xf = _layer(xf, memf, pos, l, p, batch, seq, mem_len)
    return xf.reshape(batch, seq, D_MODEL)
```
